```python
import jax
import jax.numpy as jnp
from jax import lax

D_MODEL = 1024
BATCH = 4
SEQ = 8192
DEPTH = 1
DEC_BATCH = 128
DEC_SEQ = 8
PAST_LEN = 8192
PAGE_SIZE = 128

HEAD_DIM = 64
N_HEADS_A = 8
N_HEADS_B = 8
N_HEADS_M = 4
HEAD_DIM_M = 128
N_IDX_HEADS = 8
IDX_DIM = 64
TOPK_KEYS = 256
N_MEM = 256
ROPE_THETA = 500000.0
ROT_FRACTION = 4
N_EXPERTS = 32
TOP_K = 4
D_FF = 1024
SWIGLU_LIMIT = 7.0
SWIGLU_ALPHA = 1.702
Q_BLOCK = 128
MOE_BLOCK = 128
EPS = 1e-6
FORGET_BIAS_INIT = 3.0
W_A = N_HEADS_A * HEAD_DIM
W_B = N_HEADS_B * HEAD_DIM
W_M = N_HEADS_M * HEAD_DIM_M
IDX_SCALE = (N_IDX_HEADS * IDX_DIM) ** -0.5
SPLIT_SIZES = (W_A, W_A, W_A, N_IDX_HEADS * IDX_DIM, IDX_DIM, N_IDX_HEADS,
               W_B, W_B, W_B, N_HEADS_B, W_M, D_MODEL, D_MODEL, D_MODEL)
SPLIT_POINTS = tuple(sum(SPLIT_SIZES[:i + 1]) for i in range(len(SPLIT_SIZES) - 1))
D_IN = sum(SPLIT_SIZES)

kernel_name = 'hybrid_dsa_fox_memory_moe_step'


def rmsnorm(x, g):
    xf = x.astype(jnp.float32)
    y = xf * lax.rsqrt(jnp.mean(xf * xf, axis=-1, keepdims=True) + EPS)
    return (y * g.astype(jnp.float32)).astype(x.dtype)


def rope_partial(x, pos):
    d = x.shape[-1]
    rot = d // ROT_FRACTION
    half = rot // 2
    inv_freq = ROPE_THETA ** (-jnp.arange(half, dtype=jnp.float32) * 2.0 / rot)
    ang = pos.astype(jnp.float32)[:, None] * inv_freq[None, :]
    cos = jnp.cos(ang)[:, None, :]
    sin = jnp.sin(ang)[:, None, :]
    xf = x.astype(jnp.float32)
    x1, x2 = xf[..., :half], xf[..., half:rot]
    out = jnp.concatenate([x1 * cos - x2 * sin, x2 * cos + x1 * sin, xf[..., rot:]], axis=-1)
    return out.astype(x.dtype)


def mixer_projections(x, pos, w_in, b_forget, g_mix, g_qa, g_ka, g_kidx, g_qb, g_kb, g_qm):
    B, T, _ = x.shape
    h = rmsnorm(x, g_mix)
    (qa, ka, va, qi, ki, wi, qb, kb, vb, fb, qm, ga, gb, gm) = jnp.split(h @ w_in, SPLIT_POINTS, axis=-1)
    heads = lambda t, n, d: t.reshape(B, T, n, d)
    qa = rope_partial(rmsnorm(heads(qa, N_HEADS_A, HEAD_DIM), g_qa), pos)
    ka = rope_partial(rmsnorm(heads(ka, N_HEADS_A, HEAD_DIM), g_ka), pos)
    va = heads(va, N_HEADS_A, HEAD_DIM)
    qi = rope_partial(heads(qi, N_IDX_HEADS, IDX_DIM), pos)
    ki = rope_partial(rmsnorm(ki, g_kidx)[:, :, None, :], pos)[:, :, 0, :]
    wi = wi * IDX_SCALE
    qb = rmsnorm(heads(qb, N_HEADS_B, HEAD_DIM), g_qb)
    kb = rmsnorm(heads(kb, N_HEADS_B, HEAD_DIM), g_kb)
    vb = heads(vb, N_HEADS_B, HEAD_DIM)
    logf = jax.nn.log_sigmoid((fb + b_forget).astype(jnp.float32))
    qm = rmsnorm(heads(qm, N_HEADS_M, HEAD_DIM_M), g_qm)
    return (qa, ka, va, qi, ki, wi, qb, kb, vb, logf, qm, ga, gb, gm)


def indexer_select(qi, wi, ki_all, q_pos, n_sel):
    rel = jax.nn.relu(jnp.einsum('bqhd,bld->bqhl', qi, ki_all).astype(jnp.float32))
    score = jnp.einsum('bqhl,bqh->bql', rel, wi.astype(jnp.float32))
    admissible = jnp.arange(ki_all.shape[1])[None, None, :] <= q_pos[None, :, None]
    _, idx = lax.top_k(jnp.where(admissible, score, -jnp.inf), n_sel)
    return idx, idx <= q_pos[None, :, None]


def sparse_attend(q, k_sel, v_sel, valid):
    s = jnp.einsum('bqhd,bqkhd->bqhk', q, k_sel).astype(jnp.float32) * HEAD_DIM ** -0.5
    p = jax.nn.softmax(jnp.where(valid[:, :, None, :], s, -jnp.inf), axis=-1)
    return jnp.einsum('bqhk,bqkhd->bqhd', p.astype(v_sel.dtype), v_sel)


def dsa_prompt(qa, ka, va, qi, ki, wi):
    B, S = qa.shape[:2]
    n_sel = min(TOPK_KEYS, S // 4)
    gather = jax.vmap(lambda rows, ix: rows[ix])

    def one_block(start):
        blk = lambda t: lax.dynamic_slice_in_dim(t, start, Q_BLOCK, axis=1)
        q_pos = start + jnp.arange(Q_BLOCK)
        idx, valid = indexer_select(blk(qi), blk(wi), ki, q_pos, n_sel)
        return sparse_attend(blk(qa), gather(ka, idx), gather(va, idx), valid)

    out = lax.map(one_block, jnp.arange(0, S, Q_BLOCK))
    return jnp.moveaxis(out, 0, 1).reshape(B, S, N_HEADS_A, HEAD_DIM)


def dsa_sample(qa, ka, va, qi, ki, wi, cache_a_k, cache_a_v, cache_a_kidx, page_table, layer, q_pos):
    DB, T = qa.shape[:2]
    P = page_table.shape[1] * PAGE_SIZE
    ki_past = cache_a_kidx[layer, page_table].reshape(DB, P, IDX_DIM).astype(ki.dtype)
    ki_all = jnp.concatenate([ki_past, ki], axis=1)
    idx, valid = indexer_select(qi, wi, ki_all, q_pos, min(TOPK_KEYS, (P + T) // 4))
    in_past = idx < P
    ip = jnp.minimum(idx, P - 1)
    b = jnp.arange(DB)[:, None, None]
    phys = page_table[b, ip // PAGE_SIZE]
    off = ip % PAGE_SIZE
    inew = jnp.clip(idx - P, 0, T - 1)

    def select_rows(pool, new):
        return jnp.where(in_past[..., None, None], pool[layer, phys, off].astype(new.dtype), new[b, inew])

    return sparse_attend(qa, select_rows(cache_a_k, ka), select_rows(cache_a_v, va), valid)


def forget_bias(c_q, c_k):
    return jnp.swapaxes(c_q, 1, 2)[:, :, :, None] - jnp.swapaxes(c_k, 1, 2)[:, :, None, :]


def fox_attend(q, c_q, q_pos, k, v, c_k):
    s = jnp.einsum('bqhd,blhd->bhql', q, k).astype(jnp.float32) * HEAD_DIM ** -0.5 + forget_bias(c_q, c_k)
    mask = jnp.arange(k.shape[1])[None, :] <= q_pos[:, None]
    p = jax.nn.softmax(jnp.where(mask, s, -jnp.inf), axis=-1)
    return jnp.einsum('bhql,blhd->bqhd', p.astype(v.dtype), v)


def fox_prompt(qb, kb, vb, logf):
    B, S = qb.shape[:2]
    c = jnp.cumsum(logf, axis=1)

    def one_block(start):
        blk = lambda t: lax.dynamic_slice_in_dim(t, start, Q_BLOCK, axis=1)
        return fox_attend(blk(qb), blk(c), start + jnp.arange(Q_BLOCK), kb, vb, c)

    out = lax.map(one_block, jnp.arange(0, S, Q_BLOCK))
    return jnp.moveaxis(out, 0, 1).reshape(B, S, N_HEADS_B, HEAD_DIM)


def fox_sample(qb, kb, vb, logf, cache_b_k, cache_b_v, cache_b_logf, page_table, layer, q_pos):
    DB, T = qb.shape[:2]
    P = page_table.shape[1] * PAGE_SIZE
    k_past = cache_b_k[layer, page_table].reshape(DB, P, N_HEADS_B, HEAD_DIM).astype(kb.dtype)
    v_past = cache_b_v[layer, page_table].reshape(DB, P, N_HEADS_B, HEAD_DIM).astype(vb.dtype)
    logf_past = cache_b_logf[layer, page_table].reshape(DB, P, N_HEADS_B).astype(jnp.float32)
    c = jnp.cumsum(jnp.concatenate([logf_past, logf], axis=1), axis=1)
    s = jnp.concatenate([jnp.einsum('bqhd,blhd->bhql', qb, k_past),
                         jnp.einsum('bqhd,blhd->bhql', qb, kb)], axis=-1).astype(jnp.float32)
    s = s * HEAD_DIM ** -0.5 + forget_bias(c[:, P:], c)
    mask = jnp.arange(P + T)[None, :] <= q_pos[:, None]
    p = jax.nn.softmax(jnp.where(mask, s, -jnp.inf), axis=-1).astype(vb.dtype)
    return (jnp.einsum('bhql,blhd->bqhd', p[..., :P], v_past)
            + jnp.einsum('bhql,blhd->bqhd', p[..., P:], vb))


def memory_kv(mem, g_mem, w_mem_kv, g_km):
    B, M, _ = mem.shape
    k, v = jnp.split(rmsnorm(mem, g_mem) @ w_mem_kv, 2, axis=-1)
    return (rmsnorm(k.reshape(B, M, N_HEADS_M, HEAD_DIM_M), g_km), v.reshape(B, M, N_HEADS_M, HEAD_DIM_M))


def memory_attend(q, k, v):
    s = jnp.einsum('bqhd,bmhd->bhqm', q, k.astype(q.dtype)).astype(jnp.float32) * HEAD_DIM_M ** -0.5
    p = jax.nn.softmax(s, axis=-1).astype(q.dtype)
    return jnp.einsum('bhqm,bmhd->bqhd', p, v.astype(q.dtype))


def moe(h, w_router, b_router, w_up, b_up, w_down, b_down):
    T, D = h.shape
    logits = (h @ w_router).astype(jnp.float32) + b_router.astype(jnp.float32)
    top_val, top_e = lax.top_k(logits, TOP_K)
    gate = jax.nn.softmax(top_val, axis=-1)
    n_assign = T * TOP_K
    flat_e = top_e.reshape(-1)
    order = jnp.argsort(flat_e)
    sorted_e = flat_e[order]
    counts = jnp.bincount(flat_e, length=N_EXPERTS)
    padded = (counts + MOE_BLOCK - 1) // MOE_BLOCK * MOE_BLOCK
    pad_end = jnp.cumsum(padded)
    pad_start = pad_end - padded
    start = jnp.cumsum(counts) - counts
    dest = pad_start[sorted_e] + jnp.arange(n_assign) - start[sorted_e]
    n_blocks = -(-n_assign // MOE_BLOCK) + N_EXPERTS
    tok_sorted = order // TOP_K
    row_tok = jnp.full((n_blocks * MOE_BLOCK,), T, jnp.int32).at[dest].set(tok_sorted.astype(jnp.int32))
    h_pad = jnp.concatenate([h, jnp.zeros((1, D), h.dtype)], axis=0)
    xs = h_pad[row_tok].reshape(n_blocks, MOE_BLOCK, D)
    blk_expert = jnp.minimum(jnp.searchsorted(pad_end, jnp.arange(n_blocks) * MOE_BLOCK, side='right'),
                             N_EXPERTS - 1)

    def expert_block(args):
        xb, e = args
        u = xb @ w_up[e] + b_up[e]
        glu = jnp.minimum(u[:, :D_FF], SWIGLU_LIMIT)
        lin = jnp.clip(u[:, D_FF:], -SWIGLU_LIMIT, SWIGLU_LIMIT)
        a = glu * jax.nn.sigmoid(SWIGLU_ALPHA * glu) * (lin + 1.0)
        return a @ w_down[e] + b_down[e]

    ys = lax.map(expert_block, (xs, blk_expert)).reshape(-1, D)
    y_assign = ys[dest] * gate.reshape(-1)[order][:, None].astype(ys.dtype)
    return jax.ops.segment_sum(y_assign, tok_sorted, num_segments=T)


def block_output(x, o_a, o_b, o_m, ga, gb, gm, w_pa, w_pb, w_pm, w_o, g_ffn,
                 w_router, b_router, w_up, b_up, w_down, b_down):
    B, T, D = x.shape
    merged = (jax.nn.sigmoid(ga) * (o_a.reshape(B, T, W_A) @ w_pa)
              + jax.nn.sigmoid(gb) * (o_b.reshape(B, T, W_B) @ w_pb)
              + jax.nn.sigmoid(gm) * (o_m.reshape(B, T, W_M) @ w_pm))
    x = x + merged @ w_o
    h = rmsnorm(x, g_ffn).reshape(B * T, D)
    return x + moe(h, w_router, b_router, w_up, b_up, w_down, b_down).reshape(B, T, D)


def setup_inputs(seed: int = 0) -> dict:
    key = jax.random.key(seed)
    ks = iter(jax.random.split(key, 40))
    f32 = jnp.float32

    def nrm(shape, scale=1.0):
        return scale * jax.random.normal(next(ks), shape, f32)

    def gain(shape):
        return 1.0 + 0.05 * nrm(shape)

    L = DEPTH
    n_pages = PAST_LEN // PAGE_SIZE
    n_pool = (DEC_BATCH * n_pages * 5) // 4
    perm = jax.random.permutation(next(ks), n_pool)
    page_table = perm[:DEC_BATCH * n_pages].reshape(DEC_BATCH, n_pages).astype(jnp.int32)
    return {
        'x_prompt': nrm((BATCH, SEQ, D_MODEL)),
        'x_sample': nrm((DEC_BATCH, DEC_SEQ, D_MODEL)),
        'mem_prompt': nrm((BATCH, N_MEM, D_MODEL)),
        'cache_a_k': nrm((L, n_pool, PAGE_SIZE, N_HEADS_A, HEAD_DIM)),
        'cache_a_v': nrm((L, n_pool, PAGE_SIZE, N_HEADS_A, HEAD_DIM)),
        'cache_a_kidx': nrm((L, n_pool, PAGE_SIZE, IDX_DIM)),
        'cache_b_k': nrm((L, n_pool, PAGE_SIZE, N_HEADS_B, HEAD_DIM)),
        'cache_b_v': nrm((L, n_pool, PAGE_SIZE, N_HEADS_B, HEAD_DIM)),
        'cache_b_logf': jax.nn.log_sigmoid(FORGET_BIAS_INIT + nrm((L, n_pool, PAGE_SIZE, N_HEADS_B))),
        'cache_mem_k': nrm((L, DEC_BATCH, N_MEM, N_HEADS_M, HEAD_DIM_M)),
        'cache_mem_v': nrm((L, DEC_BATCH, N_MEM, N_HEADS_M, HEAD_DIM_M)),
        'page_table': page_table,
        'w_in': nrm((L, D_MODEL, D_IN), D_MODEL ** -0.5),
        'b_forget': FORGET_BIAS_INIT + 0.1 * nrm((L, N_HEADS_B)),
        'g_mix': gain((L, D_MODEL)),
        'g_qa': gain((L, HEAD_DIM)),
        'g_ka': gain((L, HEAD_DIM)),
        'g_kidx': gain((L, IDX_DIM)),
        'g_qb': gain((L, HEAD_DIM)),
        'g_kb': gain((L, HEAD_DIM)),
        'g_qm': gain((L, HEAD_DIM_M)),
        'g_mem': gain((L, D_MODEL)),
        'w_mem_kv': nrm((L, D_MODEL, 2 * W_M), D_MODEL ** -0.5),
        'g_km': gain((L, HEAD_DIM_M)),
        'w_pa': nrm((L, W_A, D_MODEL), W_A ** -0.5),
        'w_pb': nrm((L, W_B, D_MODEL), W_B ** -0.5),
        'w_pm': nrm((L, W_M, D_MODEL), W_M ** -0.5),
        'w_o': nrm((L, D_MODEL, D_MODEL), D_MODEL ** -0.5),
        'g_ffn': gain((L, D_MODEL)),
        'w_router': nrm((L, D_MODEL, N_EXPERTS), D_MODEL ** -0.5),
        'b_router': nrm((L, N_EXPERTS), 0.01),
        'w_up': nrm((L, N_EXPERTS, D_MODEL, 2 * D_FF), D_MODEL ** -0.5),
        'b_up': nrm((L, N_EXPERTS, 2 * D_FF), 0.01),
        'w_down': nrm((L, N_EXPERTS, D_FF, D_MODEL), D_FF ** -0.5),
        'b_down': nrm((L, N_EXPERTS, D_MODEL), 0.01),
    }


def reference(x_prompt, x_sample, mem_prompt, cache_a_k, cache_a_v, cache_a_kidx, cache_b_k, cache_b_v,
              cache_b_logf, cache_mem_k, cache_mem_v, page_table, w_in, b_forget, g_mix, g_qa, g_ka,
              g_kidx, g_qb, g_kb, g_qm, g_mem, w_mem_kv, g_km, w_pa, w_pb, w_pm, w_o, g_ffn,
              w_router, b_router, w_up, b_up, w_down, b_down):
    S = x_prompt.shape[1]
    T = x_sample.shape[1]
    P = page_table.shape[1] * PAGE_SIZE
    pos_p = jnp.arange(S)
    pos_s = P + jnp.arange(T)
    hp, hs = x_prompt, x_sample
    prompt_states, sample_states = [], []
    for l in range(DEPTH):
        (qa, ka, va, qi, ki, wi, qb, kb, vb, logf, qm, ga, gb, gm) = mixer_projections(
            hp, pos_p, w_in[l], b_forget[l], g_mix[l], g_qa[l], g_ka[l], g_kidx[l], g_qb[l], g_kb[l], g_qm[l])
        mk, mv = memory_kv(mem_prompt, g_mem[l], w_mem_kv[l], g_km[l])
        o_a = dsa_prompt(qa, ka, va, qi, ki, wi)
        o_b = fox_prompt(qb, kb, vb, logf)
        o_m = memory_attend(qm, mk, mv)
        hp = block_output(hp, o_a, o_b, o_m, ga, gb, gm, w_pa[l], w_pb[l], w_pm[l], w_o[l], g_ffn[l],
                          w_router[l], b_router[l], w_up[l], b_up[l], w_down[l], b_down[l])
        prompt_states.append((ka, va, ki, kb, vb, logf, mk, mv))
        (qa, ka, va, qi, ki, wi, qb, kb, vb, logf, qm, ga, gb, gm) = mixer_projections(
            hs, pos_s, w_in[l], b_forget[l], g_mix[l], g_qa[l], g_ka[l], g_kidx[l], g_qb[l], g_kb[l], g_qm[l])
        o_a = dsa_sample(qa, ka, va, qi, ki, wi, cache_a_k, cache_a_v, cache_a_kidx, page_table, l, pos_s)
        o_b = fox_sample(qb, kb, vb, logf, cache_b_k, cache_b_v, cache_b_logf, page_table, l, pos_s)
        o_m = memory_attend(qm, cache_mem_k[l], cache_mem_v[l])
        hs = block_output(hs, o_a, o_b, o_m, ga, gb, gm, w_pa[l], w_pb[l], w_pm[l], w_o[l], g_ffn[l],
                          w_router[l], b_router[l], w_up[l], b_up[l], w_down[l], b_down[l])
        sample_states.append((ka, va, ki, kb, vb, logf))
    (pa_k, pa_v, pa_kidx, pb_k, pb_v, pb_logf, pm_k, pm_v) = [jnp.stack(s, 0) for s in zip(*prompt_states)]
    (sa_k, sa_v, sa_kidx, sb_k, sb_v, sb_logf) = [jnp.stack(s, 0) for s in zip(*sample_states)]
    return (hp, hs, pa_k, pa_v, pa_kidx, pb_k, pb_v, pb_logf, pm_k, pm_v,
            sa_k, sa_v, sa_kidx, sb_k, sb_v, sb_logf)
```

```python
import functools

import jax
import jax.numpy as jnp
from jax import lax
from jax.experimental import pallas as pl
from jax.experimental.pallas import tpu as pltpu

F32, BF16, I32 = jnp.float32, jnp.bfloat16, jnp.int32

D_MODEL = 1024
HEAD_DIM = 64
N_HEADS = 8
W_ATT = N_HEADS * HEAD_DIM
N_HEADS_M = 4
HEAD_DIM_M = 128
IDX_DIM = 64
TOPK_KEYS = 256
N_MEM = 256
ROPE_THETA = 500000.0
ROT_DIM = HEAD_DIM // 4
ROT_HALF = ROT_DIM // 2
N_EXPERTS = 32
TOP_K = 4
D_FF = 1024
SWIGLU_LIMIT = 7.0
SWIGLU_ALPHA = 1.702
PAGE = 128
EPS = 1e-6
IDX_SCALE = (N_HEADS * IDX_DIM) ** -0.5
SPLIT_SIZES = (W_ATT, W_ATT, W_ATT, N_HEADS * IDX_DIM, IDX_DIM, N_HEADS,
               W_ATT, W_ATT, W_ATT, N_HEADS, N_HEADS_M * HEAD_DIM_M, D_MODEL, D_MODEL, D_MODEL)

LANES = 128
NEG = -1e30
FLT_MAX = 3.4028234663852886e38
INT_MIN = -2147483648
BIG_IDX = 1 << 30
VMEM_LIMIT = 56 * 1024 * 1024

PROJ_TM = 256
FLASH_TQ, FLASH_TK = 256, 512
SEL_TQ, SEL_CH = 128, 512
MEM_TQ = 512
OUT_TM = 256
MOE_BLK = 256
GATHER_R = 512
COMB_TM = 256
SEL_PPS = 8
ATT_PPS = 4


def _cp(*sem):
    return pltpu.CompilerParams(dimension_semantics=sem, vmem_limit_bytes=VMEM_LIMIT)


def _dot(a, b):
    return jnp.dot(a, b, preferred_element_type=F32)


def _dot_nt(a, b):
    return lax.dot_general(a, b, (((1,), (1,)), ((), ())), preferred_element_type=F32)


def _full_spec(a):
    nd = a.ndim
    return pl.BlockSpec(a.shape, lambda *_: (0,) * nd)


def _rms_rows(x, g):
    return x * lax.rsqrt(jnp.mean(x * x, axis=-1, keepdims=True) + EPS) * g


def _proj_kernel(x_ref, gmix_ref, wbig_ref, wsm_ref, cos_ref, sa_ref, sb_ref, gqa_ref, gka_ref, gqb_ref, gkb_ref,
                 gqm_ref, gki_ref, bf_ref, bd64_ref, bd128_ref,
                 qa_o, ka_o, kab_o, va_o, vab_o, qi_o, ki_o, ki2_o, wi_o, qb_o, kb_o, kbb_o, vb_o, vbb_o, lf_o, qm_o):
    h = _rms_rows(x_ref[...], gmix_ref[...]).astype(BF16)
    cos, sa, sb = cos_ref[...], sa_ref[...], sb_ref[...]

    def seg(j):
        return _dot(h, wbig_ref[:, W_ATT * j:W_ATT * (j + 1)])

    def hnorm(y, bd_ref, g_ref):
        ms = _dot((y * y).astype(BF16), bd_ref[...])
        return y * lax.rsqrt(ms + EPS) * g_ref[...]

    def rope(y):
        parts = []
        for j in range(y.shape[1] // LANES):
            yc = y[:, LANES * j:LANES * (j + 1)]
            parts.append(yc * cos + pltpu.roll(yc, LANES - ROT_HALF, 1) * sa + pltpu.roll(yc, ROT_HALF, 1) * sb)
        return parts[0] if len(parts) == 1 else jnp.concatenate(parts, axis=1)

    qa_o[...] = rope(hnorm(seg(0), bd64_ref, gqa_ref)).astype(BF16)
    ka = rope(hnorm(seg(1), bd64_ref, gka_ref))
    ka_o[...] = ka
    kab_o[...] = ka.astype(BF16)
    va = seg(2)
    va_o[...] = va
    vab_o[...] = va.astype(BF16)
    qi_o[...] = rope(seg(3)).astype(BF16)
    qb_o[...] = hnorm(seg(4), bd64_ref, gqb_ref).astype(BF16)
    kb = hnorm(seg(5), bd64_ref, gkb_ref)
    kb_o[...] = kb
    kbb_o[...] = kb.astype(BF16)
    vb = seg(6)
    vb_o[...] = vb
    vbb_o[...] = vb.astype(BF16)
    qm_o[...] = hnorm(seg(7), bd128_ref, gqm_ref).astype(BF16)

    ys = _dot(h, wsm_ref[:, 0:LANES])
    ms = jnp.sum(ys * ys, axis=-1, keepdims=True) * (1.0 / IDX_DIM)
    ki = rope(ys * lax.rsqrt(ms + EPS) * gki_ref[...])
    ki_o[...] = ki[:, :IDX_DIM]
    ki2_o[...] = (ki + pltpu.roll(ki, IDX_DIM, 1)).astype(BF16)
    wi_o[...] = _dot(h, wsm_ref[:, LANES:2 * LANES])[:, :N_HEADS] * IDX_SCALE
    z = _dot(h, wsm_ref[:, 2 * LANES:3 * LANES]) + bf_ref[...]
    lf_o[...] = (jnp.minimum(z, 0.0) - jnp.log1p(jnp.exp(-jnp.abs(z))))[:, :N_HEADS]


def _proj(x2d, tabs, nper, wts):
    n = x2d.shape[0]
    tm = PROJ_TM
    row = lambda w: pl.BlockSpec((tm, w), lambda i: (i, 0))
    tab = pl.BlockSpec((tm, LANES), lambda i: (i % nper, 0))
    outs = [(W_ATT, BF16), (W_ATT, F32), (W_ATT, BF16), (W_ATT, F32), (W_ATT, BF16), (W_ATT, BF16), (IDX_DIM, F32),
            (LANES, BF16), (N_HEADS, F32), (W_ATT, BF16), (W_ATT, F32), (W_ATT, BF16), (W_ATT, F32), (W_ATT, BF16),
            (N_HEADS, F32), (W_ATT, BF16)]
    consts = [wts[k] for k in ("g_mix", "wbig", "wsm")] + list(tabs) + [
        wts[k] for k in ("gqa", "gka", "gqb", "gkb", "gqm", "gki", "bf", "bd64", "bd128")]
    in_specs = [row(D_MODEL)] + [_full_spec(wts["g_mix"]), _full_spec(wts["wbig"]), _full_spec(wts["wsm"]), tab, tab, tab] + [
        _full_spec(wts[k]) for k in ("gqa", "gka", "gqb", "gkb", "gqm", "gki", "bf", "bd64", "bd128")]
    names = ("qa", "ka", "ka_bf", "va", "va_bf", "qi", "ki", "ki2", "wi", "qb", "kb", "kb_bf", "vb", "vb_bf", "logf", "qm")
    res = pl.pallas_call(
        _proj_kernel, grid=(n // tm,), in_specs=in_specs,
        out_specs=[row(w) for w, _ in outs],
        out_shape=[jax.ShapeDtypeStruct((n, w), dt) for w, dt in outs],
        compiler_params=_cp("parallel"), name="proj")(x2d, *consts)
    return dict(zip(names, res))


def _rope_tables(pos):
    d = jnp.arange(LANES) % HEAD_DIM
    inv_freq = ROPE_THETA ** (-(d % ROT_HALF).astype(F32) * 2.0 / ROT_DIM)
    ang = pos.astype(F32)[:, None] * inv_freq[None, :]
    cos = jnp.where(d < ROT_DIM, jnp.cos(ang), 1.0)
    sa = jnp.where(d < ROT_HALF, -jnp.sin(ang), 0.0)
    sb = jnp.where((d >= ROT_HALF) & (d < ROT_DIM), jnp.sin(ang), 0.0)
    return cos.astype(F32), sa.astype(F32), sb.astype(F32)


def _memkv_kernel(x_ref, g_ref, w_ref, gkm_ref, bd_ref, k_o, v_o):
    h = _rms_rows(x_ref[...], g_ref[...]).astype(BF16)
    wk = N_HEADS_M * HEAD_DIM_M
    y = _dot(h, w_ref[:, :wk])
    ms = _dot((y * y).astype(BF16), bd_ref[...])
    k_o[...] = y * lax.rsqrt(ms + EPS) * gkm_ref[...]
    v_o[...] = _dot(h, w_ref[:, wk:])


def _memkv(mem2d, g_mem, w_kv, gkm4, bd128):
    n = mem2d.shape[0]
    tm = N_MEM
    wk = N_HEADS_M * HEAD_DIM_M
    row = lambda w: pl.BlockSpec((tm, w), lambda i: (i, 0))
    return pl.pallas_call(
        _memkv_kernel, grid=(n // tm,),
        in_specs=[row(D_MODEL), _full_spec(g_mem), _full_spec(w_kv), _full_spec(gkm4), _full_spec(bd128)],
        out_specs=[row(wk), row(wk)],
        out_shape=[jax.ShapeDtypeStruct((n, wk), F32)] * 2,
        compiler_params=_cp("parallel"), name="memkv")(mem2d, g_mem, w_kv, gkm4, bd128)


def _key_to_f32(k):
    bits = k ^ ((k >> 31) & jnp.int32(0x7FFFFFFF))
    return lax.bitcast_convert_type(bits, F32)


def _count(sc_ref, ngrp, rows, unroll, pred):
    def body(g, acc):
        for u in range(unroll):
            off = pl.multiple_of((g * unroll + u) * LANES, LANES)
            acc = acc + pred(sc_ref[:, pl.ds(off, LANES)], off).astype(I32)
        return acc
    acc = lax.fori_loop(0, ngrp, body, jnp.zeros((rows, LANES), I32))
    return jnp.sum(acc, axis=1, keepdims=True)


def _topk_threshold(sc_ref, ngrp, rows, unroll, n_adm, ksel, idx_bits):
    bc = lambda v: jnp.broadcast_to(v, (rows, LANES))

    def count_ge(thr):
        tb = bc(thr)
        return _count(sc_ref, ngrp, rows, unroll, lambda v, off: v >= tb)

    key = jnp.where(count_ge(jnp.zeros((rows, 1), F32)) >= ksel, 0, INT_MIN).astype(I32)

    def bit_step(it, key):
        cand = key | jnp.left_shift(jnp.int32(1), 30 - it)
        return jnp.where(count_ge(_key_to_f32(cand)) >= ksel, cand, key)

    key = lax.fori_loop(0, 31, bit_step, key)
    all_adm = n_adm <= ksel
    thr = jnp.where(all_adm, -FLT_MAX, _key_to_f32(key))
    tb = bc(thr)
    n_gt = _count(sc_ref, ngrp, rows, unroll, lambda v, off: v > tb)
    n_ge = _count(sc_ref, ngrp, rows, unroll, lambda v, off: v >= tb)
    need = ksel - n_gt
    tie = (n_ge > ksel) & jnp.logical_not(all_adm)
    lane = lax.broadcasted_iota(I32, (rows, LANES), 1)

    def tie_search(_):
        def step(it, x):
            cand = x | jnp.left_shift(jnp.int32(1), idx_bits - 1 - it)
            cb = bc(cand)
            cnt = _count(sc_ref, ngrp, rows, unroll, lambda v, off: (v == tb) & ((off + lane) < cb))
            return jnp.where(cnt < need, cand, x)
        return lax.fori_loop(0, idx_bits, step, jnp.zeros((rows, 1), I32))

    any_tie = jnp.max(tie.astype(I32)) > 0
    jcut = lax.cond(any_tie, tie_search, lambda _: jnp.full((rows, 1), BIG_IDX, I32), 0)
    return thr, jnp.where(tie, jcut, BIG_IDX)


def _selected(v, off, tb, jb, lane):
    return (v > tb) | ((v == tb) & ((off + lane) <= jb))


def _sel_prompt_kernel(seq, ksel, qi_ref, wi_ref, ki_ref, o_ref, sc_ref, wb_ref):
    tq, ch = SEL_TQ, SEL_CH
    unroll = ch // LANES
    i = pl.program_id(1)
    ngrp = ((i + 1) * tq + ch - 1) // ch
    w = wi_ref[0]
    for h in range(N_HEADS):
        wb_ref[h] = jnp.broadcast_to(w[:, h:h + 1], (tq, LANES))
    lo = lax.broadcasted_iota(I32, (tq, LANES), 1) < HEAD_DIM
    zero = jnp.zeros((tq, LANES), BF16)
    qh = []
    for pr in range(N_HEADS // 2):
        q2 = qi_ref[0, :, LANES * pr:LANES * (pr + 1)]
        qh += [jnp.where(lo, q2, zero), jnp.where(lo, zero, q2)]
    rowg = i * tq + lax.broadcasted_iota(I32, (tq, ch), 0)
    coll = lax.broadcasted_iota(I32, (tq, ch), 1)

    def chunk(c, carry):
        off = pl.multiple_of(c * ch, ch)
        kc = ki_ref[0, pl.ds(off, ch), :]
        acc = jnp.zeros((tq, ch), F32)
        for h in range(N_HEADS):
            acc = acc + jnp.maximum(_dot_nt(qh[h], kc), 0.0) * jnp.tile(wb_ref[h], (1, unroll))
        sc_ref[:, pl.ds(off, ch)] = jnp.where(off + coll <= rowg, acc, -jnp.inf)
        return carry

    lax.fori_loop(0, ngrp, chunk, 0)
    n_adm = i * tq + lax.broadcasted_iota(I32, (tq, 1), 0) + 1
    thr, jcut = _topk_threshold(sc_ref, ngrp, tq, unroll, n_adm, ksel, (seq - 1).bit_length())
    tb = jnp.broadcast_to(thr, (tq, LANES))
    jb = jnp.broadcast_to(jcut, (tq, LANES))
    lane = lax.broadcasted_iota(I32, (tq, LANES), 1)

    def write(g, carry):
        for u in range(unroll):
            off = pl.multiple_of((g * unroll + u) * LANES, LANES)
            sel = _selected(sc_ref[:, pl.ds(off, LANES)], off, tb, jb, lane)
            o_ref[0, :, pl.ds(off, LANES)] = jnp.where(sel, 0.0, NEG).astype(BF16)
        return carry

    lax.fori_loop(0, ngrp, write, 0)

    def fill(g, carry):
        off = pl.multiple_of(g * ch, ch)
        o_ref[0, :, pl.ds(off, ch)] = jnp.full((tq, ch), NEG, BF16)
        return carry

    lax.fori_loop(ngrp, seq // ch, fill, 0)


def _sel_prompt(qi, wi, ki2):
    b, seq, _ = qi.shape
    tq = SEL_TQ
    ksel = min(TOPK_KEYS, seq // 4)
    return pl.pallas_call(
        functools.partial(_sel_prompt_kernel, seq, ksel), grid=(b, seq // tq),
        in_specs=[pl.BlockSpec((1, tq, W_ATT), lambda bb, i: (bb, i, 0)),
                  pl.BlockSpec((1, tq, N_HEADS), lambda bb, i: (bb, i, 0)),
                  pl.BlockSpec((1, seq, LANES), lambda bb, i: (bb, 0, 0))],
        out_specs=pl.BlockSpec((1, tq, seq), lambda bb, i: (bb, i, 0)),
        out_shape=jax.ShapeDtypeStruct((b, seq, seq), BF16),
        scratch_shapes=[pltpu.VMEM((tq, seq), F32), pltpu.VMEM((N_HEADS, tq, LANES), F32)],
        compiler_params=_cp("parallel", "arbitrary"), name="sel_prompt")(qi, wi, ki2)


def _flash_kernel(mode, q_ref, k_ref, v_ref, b_ref, o_ref, acc_ref, m_ref, l_ref):
    tq, tk = FLASH_TQ, FLASH_TK
    i, j = pl.program_id(1), pl.program_id(2)
    last = ((i + 1) * tq - 1) // tk
    lo = lax.broadcasted_iota(I32, (tq, LANES), 1) < HEAD_DIM

    @pl.when(j == 0)
    def _():
        acc_ref[...] = jnp.zeros(acc_ref.shape, F32)
        m_ref[...] = jnp.full(m_ref.shape, NEG, F32)
        l_ref[...] = jnp.zeros(l_ref.shape, F32)

    def step(diag):
        zero = jnp.zeros((tq, LANES), BF16)
        if mode == "dsa":
            bias = b_ref[0].astype(F32)
        if diag:
            causal = (j * tk + lax.broadcasted_iota(I32, (tq, tk), 1)) <= (i * tq + lax.broadcasted_iota(I32, (tq, tk), 0))
        for pr in range(N_HEADS // 2):
            sl = slice(LANES * pr, LANES * (pr + 1))
            q2, k2, v2 = q_ref[0, :, sl], k_ref[0, :, sl], v_ref[0, :, sl]
            accp = acc_ref[:, sl]
            new = []
            for hh in range(2):
                h = 2 * pr + hh
                qh = jnp.where(lo, q2, zero) if hh == 0 else jnp.where(lo, zero, q2)
                s = _dot_nt(qh, k2)
                if mode == "dsa":
                    s = s + bias
                else:
                    s = s + b_ref[0, h:h + 1, :]
                    if diag:
                        s = jnp.where(causal, s, NEG)
                m_prev = m_ref[h]
                m_new = jnp.maximum(m_prev, jnp.max(s, axis=1, keepdims=True))
                alpha = jnp.exp(m_prev - m_new)
                p = jnp.exp(s - jnp.tile(m_new, (1, tk // LANES)))
                l_ref[h] = alpha * l_ref[h] + jnp.sum(p, axis=1, keepdims=True)
                m_ref[h] = m_new
                new.append(alpha * accp + _dot(p.astype(BF16), v2))
            acc_ref[:, sl] = jnp.where(lo, new[0], new[1])

    def finish():
        for pr in range(N_HEADS // 2):
            sl = slice(LANES * pr, LANES * (pr + 1))
            o_ref[0, :, sl] = (acc_ref[:, sl] / jnp.where(lo, l_ref[2 * pr], l_ref[2 * pr + 1])).astype(o_ref.dtype)

    if mode == "dsa":
        @pl.when(j <= last)
        def _():
            step(False)
    else:
        @pl.when(j < last)
        def _():
            step(False)

        @pl.when(j == last)
        def _():
            step(True)

    @pl.when(j == last)
    def _():
        finish()


def _flash(mode, q, k, v, bias):
    b, seq, _ = q.shape
    tq, tk = FLASH_TQ, FLASH_TK
    assert tk % tq == 0 and seq % tk == 0
    jc = lambda i, j: jnp.minimum(j, ((i + 1) * tq - 1) // tk)
    if mode == "dsa":
        bspec = pl.BlockSpec((1, tq, tk), lambda bb, i, j: (bb, i, jc(i, j)))
    else:
        bspec = pl.BlockSpec((1, N_HEADS, tk), lambda bb, i, j: (bb, 0, jc(i, j)))
    return pl.pallas_call(
        functools.partial(_flash_kernel, mode), grid=(b, seq // tq, seq // tk),
        in_specs=[pl.BlockSpec((1, tq, W_ATT), lambda bb, i, j: (bb, i, 0)),
                  pl.BlockSpec((1, tk, W_ATT), lambda bb, i, j: (bb, jc(i, j), 0)),
                  pl.BlockSpec((1, tk, W_ATT), lambda bb, i, j: (bb, jc(i, j), 0)),
                  bspec],
        out_specs=pl.BlockSpec((1, tq, W_ATT), lambda bb, i, j: (bb, i, 0)),
        out_shape=jax.ShapeDtypeStruct((b, seq, W_ATT), BF16),
        scratch_shapes=[pltpu.VMEM((tq, W_ATT), F32), pltpu.VMEM((N_HEADS, tq, LANES), F32),
                        pltpu.VMEM((N_HEADS, tq, LANES), F32)],
        compiler_params=_cp("parallel", "parallel", "arbitrary"), name="flash_" + mode)(q, k, v, bias)


def _lane_cumsum(x, lane):
    sh = 1
    while sh < LANES:
        x = x + jnp.where(lane >= sh, pltpu.roll(x, sh, 1), 0.0)
        sh *= 2
    return x


def _negcumsum_kernel(seq, x_ref, o_ref):
    lane = lax.broadcasted_iota(I32, (N_HEADS, LANES), 1)

    def body(c, carry):
        off = pl.multiple_of(c * LANES, LANES)
        cs = _lane_cumsum(x_ref[0, :, pl.ds(off, LANES)], lane) + carry
        o_ref[0, :, pl.ds(off, LANES)] = -cs
        return jnp.broadcast_to(cs[:, LANES - 1:LANES], (N_HEADS, LANES))

    lax.fori_loop(0, seq // LANES, body, jnp.zeros((N_HEADS, LANES), F32))


def _negcumsum(logf_t):
    b, _, seq = logf_t.shape
    spec = pl.BlockSpec((1, N_HEADS, seq), lambda bb: (bb, 0, 0))
    return pl.pallas_call(functools.partial(_negcumsum_kernel, seq), grid=(b,), in_specs=[spec], out_specs=spec,
                          out_shape=jax.ShapeDtypeStruct(logf_t.shape, F32),
                          compiler_params=_cp("parallel"), name="negcumsum")(logf_t)


def _memattn_kernel(q_ref, k_ref, v_ref, o_ref):
    for h in range(N_HEADS_M):
        sl = slice(HEAD_DIM_M * h, HEAD_DIM_M * (h + 1))
        kh = k_ref[0, 0, :, h, :].astype(BF16)
        vh = v_ref[0, 0, :, h, :].astype(BF16)
        s = _dot_nt(q_ref[0, :, sl], kh) * (HEAD_DIM_M ** -0.5)
        p = jnp.exp(s - jnp.max(s, axis=1, keepdims=True))
        o = _dot(p.astype(BF16), vh) / jnp.sum(p, axis=1, keepdims=True)
        o_ref[0, :, sl] = o.astype(o_ref.dtype)


def _memattn(q, k5, v5, layer, tq):
    b, t, w = q.shape
    kv = pl.BlockSpec((1, 1, N_MEM, N_HEADS_M, HEAD_DIM_M), lambda bb, i: (layer, bb, 0, 0, 0))
    qs = pl.BlockSpec((1, tq, w), lambda bb, i: (bb, i, 0))
    return pl.pallas_call(_memattn_kernel, grid=(b, t // tq), in_specs=[qs, kv, kv], out_specs=qs,
                          out_shape=jax.ShapeDtypeStruct(q.shape, BF16),
                          compiler_params=_cp("parallel", "parallel"), name="memattn")(q, k5, v5)


def _sel_sample_kernel(nsteps, pps, past, ksel, unroll, pt_ref, qi_ref, wb_ref, *refs):
    page_refs, knew_ref, o_ref, sc_ref = refs[:pps], refs[pps], refs[pps + 1], refs[pps + 2]
    j = pl.program_id(1)
    t_new = qi_ref.shape[1] // N_HEADS
    q = qi_ref[0]
    wb = wb_ref[0]

    def score(kp):
        r = jnp.maximum(_dot(q, kp.astype(BF16)), 0.0) * wb
        return jnp.sum(r.reshape(N_HEADS, t_new, LANES), axis=0)

    for r in range(pps):
        off = pl.multiple_of((j * pps + r) * PAGE, PAGE)
        sc_ref[:, pl.ds(off, PAGE)] = score(page_refs[r][0, 0])

    @pl.when(j == nsteps - 1)
    def _():
        trow = lax.broadcasted_iota(I32, (t_new, LANES), 0)
        lane = lax.broadcasted_iota(I32, (t_new, LANES), 1)
        sc_ref[:, past:past + LANES] = jnp.where(lane <= trow, score(knew_ref[0]), -jnp.inf)
        total = past + LANES
        n_adm = past + lax.broadcasted_iota(I32, (t_new, 1), 0) + 1
        thr, jcut = _topk_threshold(sc_ref, total // (LANES * unroll), t_new, unroll, n_adm, ksel, (total - 1).bit_length())
        tb = jnp.broadcast_to(thr, (t_new, LANES))
        jb = jnp.broadcast_to(jcut, (t_new, LANES))
        for c in range(total // LANES):
            off = c * LANES
            sel = _selected(sc_ref[:, off:off + LANES], off, tb, jb, lane)
            o_ref[0, :, off:off + LANES] = jnp.where(sel, 0.0, NEG)


def _sel_sample(pt_flat, qi_hs, wb_hs, cache_kidx, layer, knew, npg):
    db = qi_hs.shape[0]
    t_new = qi_hs.shape[1] // N_HEADS
    past = npg * PAGE
    total = past + LANES
    pps = SEL_PPS if npg % SEL_PPS == 0 else 1
    nsteps = npg // pps
    ncol = total // LANES
    unroll = max(u for u in range(1, 9) if ncol % u == 0)
    ksel = min(TOPK_KEYS, (past + t_new) // 4)
    page = lambda r: pl.BlockSpec((1, 1, IDX_DIM, PAGE), lambda b, j, pt: (layer, pt[b * npg + j * pps + r], 0, 0))
    return pl.pallas_call(
        functools.partial(_sel_sample_kernel, nsteps, pps, past, ksel, unroll),
        grid_spec=pltpu.PrefetchScalarGridSpec(
            num_scalar_prefetch=1, grid=(db, nsteps),
            in_specs=[pl.BlockSpec((1, N_HEADS * t_new, IDX_DIM), lambda b, j, pt: (b, 0, 0)),
                      pl.BlockSpec((1, N_HEADS * t_new, LANES), lambda b, j, pt: (b, 0, 0))]
                     + [page(r) for r in range(pps)]
                     + [pl.BlockSpec((1, IDX_DIM, PAGE), lambda b, j, pt: (b, 0, 0))],
            out_specs=pl.BlockSpec((1, t_new, total), lambda b, j, pt: (b, 0, 0)),
            scratch_shapes=[pltpu.VMEM((t_new, total), F32)]),
        out_shape=jax.ShapeDtypeStruct((db, t_new, total), F32),
        compiler_params=_cp("parallel", "arbitrary"), name="sel_sample",
    )(pt_flat, qi_hs, wb_hs, *([cache_kidx] * pps), knew)


def _paged_kernel(mode, nsteps, pps, pt_ref, q_ref, *refs):
    k_refs, v_refs, b_refs = refs[:pps], refs[pps:2 * pps], refs[2 * pps:3 * pps]
    kn_ref, vn_ref, bn_ref, o_ref, acc_ref, m_ref, l_ref, coff_ref = refs[3 * pps:]
    j = pl.program_id(1)
    t_new = q_ref.shape[2]
    lane = lax.broadcasted_iota(I32, (t_new, LANES), 1)
    trow = lax.broadcasted_iota(I32, (t_new, LANES), 0)

    @pl.when(j == 0)
    def _():
        acc_ref[...] = jnp.zeros(acc_ref.shape, F32)
        m_ref[...] = jnp.full(m_ref.shape, NEG, F32)
        l_ref[...] = jnp.zeros(l_ref.shape, F32)
        coff_ref[...] = jnp.zeros(coff_ref.shape, F32)

    def attend(k_head, v_head, bias_fn):
        for h in range(N_HEADS):
            s = _dot(q_ref[0, h], k_head(h).astype(BF16)) + bias_fn(h)
            m_prev = m_ref[h]
            m_new = jnp.maximum(m_prev, jnp.max(s, axis=1, keepdims=True))
            alpha = jnp.exp(m_prev - m_new)
            p = jnp.exp(s - m_new)
            l_ref[h] = alpha * l_ref[h] + jnp.sum(p, axis=1, keepdims=True)
            m_ref[h] = m_new
            acc_ref[h] = alpha[:, :HEAD_DIM] * acc_ref[h] + _dot_nt(p.astype(BF16), v_head(h).astype(BF16))

    def neg_cum(lf):
        c = _lane_cumsum(lf, lane) + coff_ref[...]
        coff_ref[...] = jnp.broadcast_to(c[:, LANES - 1:LANES], (N_HEADS, LANES))
        return -c

    for r in range(pps):
        kr, vr = k_refs[r], v_refs[r]
        if mode == "fox":
            nc = neg_cum(b_refs[r][0, 0])
            bias_fn = lambda h, nc=nc: nc[h:h + 1, :]
        else:
            bias_fn = lambda h, br=b_refs[r]: br[0]
        attend(lambda h, kr=kr: kr[0, 0, h], lambda h, vr=vr: vr[0, 0, h], bias_fn)

    @pl.when(j == nsteps - 1)
    def _():
        if mode == "fox":
            nc = neg_cum(bn_ref[0])
            bias_fn = lambda h: jnp.where(lane <= trow, nc[h:h + 1, :], NEG)
        else:
            bias_fn = lambda h: bn_ref[0]
        attend(lambda h: kn_ref[0, h], lambda h: vn_ref[0, h], bias_fn)
        for h in range(N_HEADS):
            o_ref[0, h] = (acc_ref[h] / l_ref[h][:, :HEAD_DIM]).astype(o_ref.dtype)


def _paged(mode, pt_flat, q_hs, cache_k, cache_v, layer, bias_src, knew, vnew, bias_new, npg):
    db, _, t_new, _ = q_hs.shape
    assert t_new == N_HEADS
    pps = ATT_PPS if npg % ATT_PPS == 0 else 1
    nsteps = npg // pps
    pidx = lambda r: (lambda b, j, pt: (layer, pt[b * npg + j * pps + r], 0, 0, 0))
    page = lambda r: pl.BlockSpec((1, 1, N_HEADS, HEAD_DIM, PAGE), pidx(r))
    if mode == "fox":
        bspec = lambda r: pl.BlockSpec((1, 1, N_HEADS, PAGE), lambda b, j, pt: (layer, pt[b * npg + j * pps + r], 0, 0))
        bnew = pl.BlockSpec((1, N_HEADS, LANES), lambda b, j, pt: (b, 0, 0))
    else:
        bspec = lambda r: pl.BlockSpec((1, t_new, PAGE), lambda b, j, pt: (b, 0, j * pps + r))
        bnew = pl.BlockSpec((1, t_new, LANES), lambda b, j, pt: (b, 0, npg))
    newp = pl.BlockSpec((1, N_HEADS, HEAD_DIM, PAGE), lambda b, j, pt: (b, 0, 0, 0))
    qs = pl.BlockSpec((1, N_HEADS, t_new, HEAD_DIM), lambda b, j, pt: (b, 0, 0, 0))
    return pl.pallas_call(
        functools.partial(_paged_kernel, mode, nsteps, pps),
        grid_spec=pltpu.PrefetchScalarGridSpec(
            num_scalar_prefetch=1, grid=(db, nsteps),
            in_specs=[qs] + [page(r) for r in range(pps)] + [page(r) for r in range(pps)]
                     + [bspec(r) for r in range(pps)] + [newp, newp, bnew],
            out_specs=qs,
            scratch_shapes=[pltpu.VMEM((N_HEADS, t_new, HEAD_DIM), F32), pltpu.VMEM((N_HEADS, t_new, LANES), F32),
                            pltpu.VMEM((N_HEADS, t_new, LANES), F32), pltpu.VMEM((N_HEADS, LANES), F32)]),
        out_shape=jax.ShapeDtypeStruct(q_hs.shape, BF16),
        compiler_params=_cp("parallel", "arbitrary"), name="paged_" + mode,
    )(pt_flat, q_hs, *([cache_k] * pps), *([cache_v] * pps), *([bias_src] * pps), knew, vnew, bias_new)


def _blockout_kernel(x_ref, oa_ref, ob_ref, om_ref, gmix_ref, wg_ref, wpa_ref, wpb_ref, wpm_ref, wo_ref, gffn_ref,
                     wr_ref, br_ref, x1_o, h2_o, te_o, tg_o):
    x = x_ref[...]
    tm = x.shape[0]
    h = _rms_rows(x, gmix_ref[...]).astype(BF16)
    merged = jnp.zeros((tm, D_MODEL), F32)
    for n, (o_ref, wp_ref) in enumerate(((oa_ref, wpa_ref), (ob_ref, wpb_ref), (om_ref, wpm_ref))):
        gate = jax.nn.sigmoid(_dot(h, wg_ref[:, D_MODEL * n:D_MODEL * (n + 1)]))
        merged = merged + gate * _dot(o_ref[...], wp_ref[...])
    x1 = x + _dot(merged.astype(BF16), wo_ref[...])
    x1_o[...] = x1
    h2 = _rms_rows(x1, gffn_ref[...])
    for c in range(D_MODEL // LANES):
        h2_o[:, c, :] = h2[:, LANES * c:LANES * (c + 1)]
    logits = _dot(h2.astype(BF16), wr_ref[...]) + br_ref[...]
    lane = lax.broadcasted_iota(I32, (tm, LANES), 1)
    vals, te = [], jnp.zeros((tm, LANES), I32)
    for k in range(TOP_K):
        m = jnp.max(logits, axis=1, keepdims=True)
        ix = jnp.min(jnp.where(logits == m, lane, LANES), axis=1, keepdims=True)
        vals.append(m)
        te = jnp.where(lane == k, ix, te)
        logits = jnp.where(lane == ix, -jnp.inf, logits)
    ex = [jnp.exp(v - vals[0]) for v in vals]
    den = ex[0] + ex[1] + ex[2] + ex[3]
    tg = jnp.zeros((tm, LANES), F32)
    for k in range(TOP_K):
        tg = jnp.where(lane == k, ex[k] / den, tg)
    te_o[...] = te
    tg_o[...] = tg


def _blockout(x2d, oa, ob, om, wts):
    n = x2d.shape[0]
    tm = OUT_TM
    row = lambda w: pl.BlockSpec((tm, w), lambda i: (i, 0))
    names = ("g_mix", "wg", "wpa", "wpb", "wpm", "wo", "g_ffn", "wr", "br")
    nch = D_MODEL // LANES
    return pl.pallas_call(
        _blockout_kernel, grid=(n // tm,),
        in_specs=[row(D_MODEL), row(W_ATT), row(W_ATT), row(N_HEADS_M * HEAD_DIM_M)] + [_full_spec(wts[k]) for k in names],
        out_specs=[row(D_MODEL), pl.BlockSpec((tm, nch, LANES), lambda i: (i, 0, 0)), row(LANES), row(LANES)],
        out_shape=[jax.ShapeDtypeStruct((n, D_MODEL), F32), jax.ShapeDtypeStruct((n, nch, LANES), F32),
                   jax.ShapeDtypeStruct((n, LANES), I32), jax.ShapeDtypeStruct((n, LANES), F32)],
        compiler_params=_cp("parallel"), name="blockout")(x2d, oa, ob, om, *[wts[k] for k in names])


def _row_copy(src_ref, dst_ref, src_row, dst_row, sem):
    return pltpu.make_async_copy(src_ref.at[src_row], dst_ref.at[dst_row], sem)


def _gather_kernel(idx_ref, src_ref, out_ref, sem):
    base = pl.program_id(0) * GATHER_R

    def issue(r, c):
        _row_copy(src_ref, out_ref, idx_ref[r], base + r, sem).start()
        return c

    lax.fori_loop(0, GATHER_R, issue, 0)

    def wait(r, c):
        _row_copy(src_ref, out_ref, 0, base + r, sem).wait()
        return c

    lax.fori_loop(0, GATHER_R, wait, 0)


def _gather_rows(idx, src):
    n = idx.shape[0]
    return pl.pallas_call(
        _gather_kernel, grid=(n // GATHER_R,),
        in_specs=[pl.BlockSpec((GATHER_R,), lambda i: (i,), memory_space=pltpu.SMEM),
                  pl.BlockSpec(memory_space=pl.ANY)],
        out_specs=pl.BlockSpec(memory_space=pl.ANY),
        out_shape=jax.ShapeDtypeStruct((n,) + src.shape[1:], src.dtype),
        scratch_shapes=[pltpu.SemaphoreType.DMA(())],
        compiler_params=_cp("arbitrary"), name="moe_gather")(idx, src)


def _ffn_kernel(be_ref, na_ref, x_ref, wu_ref, bu_ref, wd_ref, bd_ref, y_ref):
    nch = D_MODEL // LANES
    i = pl.program_id(0)

    @pl.when(i < na_ref[0])
    def _():
        x = jnp.concatenate([x_ref[:, c, :] for c in range(nch)], axis=1).astype(BF16)
        u = _dot(x, wu_ref[0]) + bu_ref[0]
        glu = jnp.minimum(u[:, :D_FF], SWIGLU_LIMIT)
        lin = jnp.clip(u[:, D_FF:], -SWIGLU_LIMIT, SWIGLU_LIMIT)
        a = glu * jax.nn.sigmoid(SWIGLU_ALPHA * glu) * (lin + 1.0)
        y = _dot(a.astype(BF16), wd_ref[0]) + bd_ref[0]
        for c in range(nch):
            y_ref[:, c, :] = y[:, LANES * c:LANES * (c + 1)]

    @pl.when(i >= na_ref[0])
    def _():
        y_ref[...] = jnp.zeros(y_ref.shape, F32)


def _ffn(blk_expert, n_active, xs, w_up, b_up, w_down, b_down):
    n = xs.shape[0]
    nch = D_MODEL // LANES
    xspec = pl.BlockSpec((MOE_BLK, nch, LANES), lambda i, be, na: (i, 0, 0))
    ex = lambda shape: pl.BlockSpec((1,) + shape, lambda i, be, na: (be[i], 0, 0))
    return pl.pallas_call(
        _ffn_kernel,
        grid_spec=pltpu.PrefetchScalarGridSpec(
            num_scalar_prefetch=2, grid=(n // MOE_BLK,),
            in_specs=[xspec, ex((D_MODEL, 2 * D_FF)), ex((1, 2 * D_FF)), ex((D_FF, D_MODEL)), ex((1, D_MODEL))],
            out_specs=xspec),
        out_shape=jax.ShapeDtypeStruct(xs.shape, F32),
        compiler_params=_cp("arbitrary"), name="moe_ffn")(blk_expert, n_active, xs, w_up, b_up, w_down, b_down)


def _combine_kernel(idx_ref, ys_ref, x1_ref, g_ref, o_ref, buf_ref, sem):
    tm = COMB_TM
    nch = D_MODEL // LANES

    def issue(r, c):
        _row_copy(ys_ref, buf_ref, idx_ref[r], r, sem).start()
        return c

    lax.fori_loop(0, TOP_K * tm, issue, 0)

    def wait(r, c):
        _row_copy(ys_ref, buf_ref, 0, r, sem).wait()
        return c

    lax.fori_loop(0, TOP_K * tm, wait, 0)
    g = g_ref[...]
    y = x1_ref[...]
    for k in range(TOP_K):
        rows = jnp.concatenate([buf_ref[k * tm:(k + 1) * tm, c, :] for c in range(nch)], axis=1)
        y = y + g[:, k:k + 1] * rows
    o_ref[...] = y


def _combine(dest_tiles, ys, x1, gates):
    n = x1.shape[0]
    tm = COMB_TM
    row = lambda w: pl.BlockSpec((tm, w), lambda i: (i, 0))
    return pl.pallas_call(
        _combine_kernel, grid=(n // tm,),
        in_specs=[pl.BlockSpec((TOP_K * tm,), lambda i: (i,), memory_space=pltpu.SMEM),
                  pl.BlockSpec(memory_space=pl.ANY), row(D_MODEL), row(LANES)],
        out_specs=row(D_MODEL),
        out_shape=jax.ShapeDtypeStruct((n, D_MODEL), F32),
        scratch_shapes=[pltpu.VMEM((TOP_K * tm,) + ys.shape[1:], F32), pltpu.SemaphoreType.DMA(())],
        compiler_params=_cp("arbitrary"), name="moe_combine")(dest_tiles, ys, x1, gates)


def _moe(x1, h2, top_e, top_g, wts):
    n = x1.shape[0]
    n_assign = n * TOP_K
    flat_e = top_e[:, :TOP_K].reshape(-1)
    onehot = (flat_e[:, None] == jnp.arange(N_EXPERTS, dtype=I32)[None, :]).astype(I32)
    csum = jnp.cumsum(onehot, axis=0)
    rank = jnp.take_along_axis(csum, flat_e[:, None], axis=1)[:, 0] - 1
    counts = csum[-1]
    padded = (counts + MOE_BLK - 1) // MOE_BLK * MOE_BLK
    pad_end = jnp.cumsum(padded)
    dest = (pad_end - padded)[flat_e] + rank
    n_blocks = -(-n_assign // MOE_BLK) + N_EXPERTS
    n_rows = -(-(n_blocks * MOE_BLK) // GATHER_R) * GATHER_R
    row_tok = jnp.zeros((n_rows,), I32).at[dest].set(jnp.arange(n_assign, dtype=I32) // TOP_K)
    nb = n_rows // MOE_BLK
    blk_expert = jnp.minimum(jnp.searchsorted(pad_end, jnp.arange(nb, dtype=I32) * MOE_BLK, side="right"),
                             N_EXPERTS - 1).astype(I32)
    n_active = (pad_end[-1:] // MOE_BLK).astype(I32)
    xs = _gather_rows(row_tok, h2)
    ys = _ffn(blk_expert, n_active, xs, wts["w_up"], wts["b_up"], wts["w_down"], wts["b_down"])
    dest_tiles = dest.reshape(n // COMB_TM, COMB_TM, TOP_K).transpose(0, 2, 1).reshape(-1)
    return _combine(dest_tiles, ys, x1, top_g)


def _block_diag(width, blk):
    r = jnp.arange(width) // blk
    return jnp.where(r[:, None] == r[None, :], 1.0 / blk, 0.0).astype(BF16)


def _pad_cols(a, width):
    return jnp.pad(a, ((0, 0), (0, width - a.shape[1])))


def _layer_weights(l, w_in, b_forget, g_mix, g_qa, g_ka, g_kidx, g_qb, g_kb, g_qm, g_mem, w_mem_kv, g_km, w_pa, w_pb,
                   w_pm, w_o, g_ffn, w_router, b_router, w_up, b_up, w_down, b_down):
    pts = [0]
    for s in SPLIT_SIZES:
        pts.append(pts[-1] + s)
    col = lambda n: w_in[l][:, pts[n]:pts[n + 1]]
    qa, ka, va, qi, ki, wi, qb, kb, vb, fb, qm, ga, gb, gm = [col(n) for n in range(len(SPLIT_SIZES))]
    rowv = lambda v: v.astype(F32)[None, :]
    scale = HEAD_DIM ** -0.5
    return dict(
        wbig=jnp.concatenate([qa, ka, va, qi, qb, kb, vb, qm], axis=1).astype(BF16),
        wsm=jnp.concatenate([_pad_cols(ki, LANES), _pad_cols(wi, LANES), _pad_cols(fb, LANES)], axis=1).astype(BF16),
        wg=jnp.concatenate([ga, gb, gm], axis=1).astype(BF16),
        g_mix=rowv(g_mix[l]), g_ffn=rowv(g_ffn[l]), g_mem=rowv(g_mem[l]),
        gqa=rowv(jnp.tile(g_qa[l] * scale, N_HEADS)), gka=rowv(jnp.tile(g_ka[l], N_HEADS)),
        gqb=rowv(jnp.tile(g_qb[l] * scale, N_HEADS)), gkb=rowv(jnp.tile(g_kb[l], N_HEADS)),
        gqm=rowv(jnp.tile(g_qm[l], N_HEADS_M)), gkm=rowv(jnp.tile(g_km[l], N_HEADS_M)),
        gki=_pad_cols(rowv(g_kidx[l]), LANES), bf=_pad_cols(rowv(b_forget[l]), LANES),
        bd64=_block_diag(W_ATT, HEAD_DIM), bd128=_block_diag(N_HEADS_M * HEAD_DIM_M, HEAD_DIM_M),
        w_mem_kv=w_mem_kv[l].astype(BF16),
        wpa=w_pa[l].astype(BF16), wpb=w_pb[l].astype(BF16), wpm=w_pm[l].astype(BF16), wo=w_o[l].astype(BF16),
        wr=_pad_cols(w_router[l], LANES).astype(BF16),
        br=jnp.pad(rowv(b_router[l]), ((0, 0), (0, LANES - N_EXPERTS)), constant_values=NEG),
        w_up=w_up[l].astype(BF16), b_up=b_up[l].astype(F32)[:, None, :],
        w_down=w_down[l].astype(BF16), b_down=b_down[l].astype(F32)[:, None, :],
    )


def _finish_block(x2d, oa, ob, om, wts):
    x1, h2, te, tg = _blockout(x2d, oa, ob, om, wts)
    return _moe(x1, h2, te, tg, wts)


def kernel(x_prompt, x_sample, mem_prompt, cache_a_k, cache_a_v, cache_a_kidx, cache_b_k, cache_b_v, cache_b_logf,
           cache_mem_k, cache_mem_v, page_table, w_in, b_forget, g_mix, g_qa, g_ka, g_kidx, g_qb, g_kb, g_qm, g_mem,
           w_mem_kv, g_km, w_pa, w_pb, w_pm, w_o, g_ffn, w_router, b_router, w_up, b_up, w_down, b_down):
    depth = w_in.shape[0]
    b, seq, _ = x_prompt.shape
    db, t_new, _ = x_sample.shape
    npg = page_table.shape[1]
    past = npg * PAGE
    pt_flat = page_table.reshape(-1).astype(I32)
    tabs_p = _rope_tables(jnp.arange(seq))
    tabs_s = _rope_tables(past + jnp.arange(PROJ_TM) % t_new)
    page_t = lambda c: jnp.transpose(c, (0, 1, 3, 4, 2))
    cak_t, cav_t, cbk_t, cbv_t = page_t(cache_a_k), page_t(cache_a_v), page_t(cache_b_k), page_t(cache_b_v)
    kidx_t = jnp.swapaxes(cache_a_kidx, 2, 3)
    logf_cache_t = jnp.swapaxes(cache_b_logf, 2, 3)
    hp = x_prompt.reshape(b * seq, D_MODEL)
    hs = x_sample.reshape(db * t_new, D_MODEL)
    p_states, s_states = [], []
    heads = lambda t, nb, nt: t.reshape(nb, nt, N_HEADS, HEAD_DIM)
    for l in range(depth):
        wts = _layer_weights(l, w_in, b_forget, g_mix, g_qa, g_ka, g_kidx, g_qb, g_kb, g_qm, g_mem, w_mem_kv, g_km,
                             w_pa, w_pb, w_pm, w_o, g_ffn, w_router, b_router, w_up, b_up, w_down, b_down)
        pr = _proj(hp, tabs_p, seq // PROJ_TM, wts)
        mk, mv = _memkv(mem_prompt.reshape(b * N_MEM, D_MODEL), wts["g_mem"], wts["w_mem_kv"], wts["gkm"], wts["bd128"])
        mk5 = mk.reshape(1, b, N_MEM, N_HEADS_M, HEAD_DIM_M)
        mv5 = mv.reshape(1, b, N_MEM, N_HEADS_M, HEAD_DIM_M)
        r3 = lambda t: t.reshape(b, seq, t.shape[-1])
        sel_bias = _sel_prompt(r3(pr["qi"]), r3(pr["wi"]), r3(pr["ki2"]))
        o_a = _flash("dsa", r3(pr["qa"]), r3(pr["ka_bf"]), r3(pr["va_bf"]), sel_bias)
        negc = _negcumsum(jnp.swapaxes(r3(pr["logf"]), 1, 2))
        o_b = _flash("fox", r3(pr["qb"]), r3(pr["kb_bf"]), r3(pr["vb_bf"]), negc)
        o_m = _memattn(r3(pr["qm"]), mk5, mv5, 0, MEM_TQ)
        rf = lambda t: t.reshape(b * seq, t.shape[-1])
        p_states.append((heads(pr["ka"], b, seq), heads(pr["va"], b, seq), r3(pr["ki"]), heads(pr["kb"], b, seq),
                         heads(pr["vb"], b, seq), r3(pr["logf"]), mk5[0], mv5[0]))
        hp = _finish_block(hp, rf(o_a), rf(o_b), rf(o_m), wts)
        sr = _proj(hs, tabs_s, 1, wts)
        hs_t = lambda t, d: jnp.swapaxes(t.reshape(db, t_new, N_HEADS, d), 1, 2)
        new_page = lambda t: jnp.pad(jnp.transpose(heads(t, db, t_new), (0, 2, 3, 1)),
                                     ((0, 0), (0, 0), (0, 0), (0, PAGE - t_new)))
        qi_hs = hs_t(sr["qi"], IDX_DIM).reshape(db, N_HEADS * t_new, IDX_DIM)
        wb_hs = jnp.broadcast_to(jnp.swapaxes(sr["wi"].reshape(db, t_new, N_HEADS), 1, 2).reshape(db, N_HEADS * t_new, 1),
                                 (db, N_HEADS * t_new, LANES))
        ki_new = jnp.pad(jnp.swapaxes(sr["ki"].reshape(db, t_new, IDX_DIM), 1, 2), ((0, 0), (0, 0), (0, PAGE - t_new)))
        sel_s = _sel_sample(pt_flat, qi_hs, wb_hs, kidx_t, l, ki_new, npg)
        o_a = _paged("dsa", pt_flat, hs_t(sr["qa"], HEAD_DIM), cak_t, cav_t, l, sel_s,
                     new_page(sr["ka"]), new_page(sr["va"]), sel_s, npg)
        lf_new = jnp.pad(jnp.swapaxes(sr["logf"].reshape(db, t_new, N_HEADS), 1, 2), ((0, 0), (0, 0), (0, LANES - t_new)))
        o_b = _paged("fox", pt_flat, hs_t(sr["qb"], HEAD_DIM), cbk_t, cbv_t, l, logf_cache_t,
                     new_page(sr["kb"]), new_page(sr["vb"]), lf_new, npg)
        o_m = _memattn(sr["qm"].reshape(db, t_new, -1), cache_mem_k, cache_mem_v, l, t_new)
        un_hs = lambda o: jnp.swapaxes(o, 1, 2).reshape(db * t_new, W_ATT)
        s_states.append((heads(sr["ka"], db, t_new), heads(sr["va"], db, t_new), sr["ki"].reshape(db, t_new, IDX_DIM),
                         heads(sr["kb"], db, t_new), heads(sr["vb"], db, t_new), sr["logf"].reshape(db, t_new, N_HEADS)))
        hs = _finish_block(hs, un_hs(o_a), un_hs(o_b), o_m.reshape(db * t_new, -1), wts)
    stack = lambda states: [jnp.stack(s, 0) for s in zip(*states)]
    return (hp.reshape(b, seq, D_MODEL), hs.reshape(db, t_new, D_MODEL), *stack(p_states), *stack(s_states))
```

```python
import functools

import jax
import jax.numpy as jnp
from jax import lax
from jax.experimental import pallas as pl
from jax.experimental.pallas import tpu as pltpu

F32, BF16, I32 = jnp.float32, jnp.bfloat16, jnp.int32

D_MODEL = 1024
HEAD_DIM = 64
N_HEADS = 8
W_ATT = N_HEADS * HEAD_DIM
N_HEADS_M = 4
HEAD_DIM_M = 128
IDX_DIM = 64
TOPK_KEYS = 256
N_MEM = 256
ROPE_THETA = 500000.0
ROT_DIM = HEAD_DIM // 4
ROT_HALF = ROT_DIM // 2
N_EXPERTS = 32
TOP_K = 4
D_FF = 1024
SWIGLU_LIMIT = 7.0
SWIGLU_ALPHA = 1.702
PAGE = 128
EPS = 1e-6
IDX_SCALE = (N_HEADS * IDX_DIM) ** -0.5
SPLIT_SIZES = (W_ATT, W_ATT, W_ATT, N_HEADS * IDX_DIM, IDX_DIM, N_HEADS,
               W_ATT, W_ATT, W_ATT, N_HEADS, N_HEADS_M * HEAD_DIM_M, D_MODEL, D_MODEL, D_MODEL)

LANES = 128
NEG = -1e30
FLT_MAX = 3.4028234663852886e38
INT_MIN = -2147483648
BIG_IDX = 1 << 30
VMEM_LIMIT = 56 * 1024 * 1024

PROJ_TM = 256
FLASH_TQ, FLASH_TK = 256, 512
FLASH_SQ, FLASH_SK = 256, 512
SEL_TQ, SEL_CH = 128, 512
MEM_TQ = 512
OUT_TM = 256
MOE_BLK = 256
GATHER_R = 512
COMB_TM = 256
SEL_PPS = 8
ATT_PPS = 8


def _cp(*sem):
    return pltpu.CompilerParams(dimension_semantics=sem, vmem_limit_bytes=VMEM_LIMIT)


def _dot(a, b):
    return jnp.dot(a, b, preferred_element_type=F32)


def _dot_nt(a, b):
    return lax.dot_general(a, b, (((1,), (1,)), ((), ())), preferred_element_type=F32)


def _full_spec(a):
    nd = a.ndim
    return pl.BlockSpec(a.shape, lambda *_: (0,) * nd)


def _rms_rows(x, g):
    return x * lax.rsqrt(jnp.mean(x * x, axis=-1, keepdims=True) + EPS) * g


def _proj_kernel(x_ref, gmix_ref, wbig_ref, wsm_ref, cos_ref, sa_ref, sb_ref, gqa_ref, gka_ref, gqb_ref, gkb_ref,
                 gqm_ref, gki_ref, bf_ref, bd64_ref, bd128_ref,
                 qa_o, ka_o, kab_o, va_o, vab_o, qi_o, ki_o, ki2_o, wi_o, qb_o, kb_o, kbb_o, vb_o, vbb_o, lf_o, qm_o):
    h = _rms_rows(x_ref[...], gmix_ref[...]).astype(BF16)
    cos, sa, sb = cos_ref[...], sa_ref[...], sb_ref[...]

    def seg(j):
        return _dot(h, wbig_ref[:, W_ATT * j:W_ATT * (j + 1)])

    def hnorm(y, bd_ref, g_ref):
        ms = _dot((y * y).astype(BF16), bd_ref[...])
        return y * lax.rsqrt(ms + EPS) * g_ref[...]

    def rope(y):
        parts = []
        for j in range(y.shape[1] // LANES):
            yc = y[:, LANES * j:LANES * (j + 1)]
            parts.append(yc * cos + pltpu.roll(yc, LANES - ROT_HALF, 1) * sa + pltpu.roll(yc, ROT_HALF, 1) * sb)
        return parts[0] if len(parts) == 1 else jnp.concatenate(parts, axis=1)

    qa_o[...] = rope(hnorm(seg(0), bd64_ref, gqa_ref)).astype(BF16)
    ka = rope(hnorm(seg(1), bd64_ref, gka_ref))
    ka_o[...] = ka
    kab_o[...] = ka.astype(BF16)
    va = seg(2)
    va_o[...] = va
    vab_o[...] = va.astype(BF16)
    qi_o[...] = rope(seg(3)).astype(BF16)
    qb_o[...] = hnorm(seg(4), bd64_ref, gqb_ref).astype(BF16)
    kb = hnorm(seg(5), bd64_ref, gkb_ref)
    kb_o[...] = kb
    kbb_o[...] = kb.astype(BF16)
    vb = seg(6)
    vb_o[...] = vb
    vbb_o[...] = vb.astype(BF16)
    qm_o[...] = hnorm(seg(7), bd128_ref, gqm_ref).astype(BF16)

    ys = _dot(h, wsm_ref[:, 0:LANES])
    ms = jnp.sum(ys * ys, axis=-1, keepdims=True) * (1.0 / IDX_DIM)
    ki = rope(ys * lax.rsqrt(ms + EPS) * gki_ref[...])
    ki_o[...] = ki[:, :IDX_DIM]
    ki2_o[...] = (ki + pltpu.roll(ki, IDX_DIM, 1)).astype(BF16)
    wi_o[...] = _dot(h, wsm_ref[:, LANES:2 * LANES])[:, :N_HEADS] * IDX_SCALE
    z = _dot(h, wsm_ref[:, 2 * LANES:3 * LANES]) + bf_ref[...]
    lf_o[...] = (jnp.minimum(z, 0.0) - jnp.log1p(jnp.exp(-jnp.abs(z))))[:, :N_HEADS]


def _proj(x2d, tabs, nper, wts):
    n = x2d.shape[0]
    tm = PROJ_TM
    row = lambda w: pl.BlockSpec((tm, w), lambda i: (i, 0))
    tab = pl.BlockSpec((tm, LANES), lambda i: (i % nper, 0))
    outs = [(W_ATT, BF16), (W_ATT, F32), (W_ATT, BF16), (W_ATT, F32), (W_ATT, BF16), (W_ATT, BF16), (IDX_DIM, F32),
            (LANES, BF16), (N_HEADS, F32), (W_ATT, BF16), (W_ATT, F32), (W_ATT, BF16), (W_ATT, F32), (W_ATT, BF16),
            (N_HEADS, F32), (W_ATT, BF16)]
    consts = [wts[k] for k in ("g_mix", "wbig", "wsm")] + list(tabs) + [
        wts[k] for k in ("gqa", "gka", "gqb", "gkb", "gqm", "gki", "bf", "bd64", "bd128")]
    in_specs = [row(D_MODEL)] + [_full_spec(wts["g_mix"]), _full_spec(wts["wbig"]), _full_spec(wts["wsm"]), tab, tab, tab] + [
        _full_spec(wts[k]) for k in ("gqa", "gka", "gqb", "gkb", "gqm", "gki", "bf", "bd64", "bd128")]
    names = ("qa", "ka", "ka_bf", "va", "va_bf", "qi", "ki", "ki2", "wi", "qb", "kb", "kb_bf", "vb", "vb_bf", "logf", "qm")
    res = pl.pallas_call(
        _proj_kernel, grid=(n // tm,), in_specs=in_specs,
        out_specs=[row(w) for w, _ in outs],
        out_shape=[jax.ShapeDtypeStruct((n, w), dt) for w, dt in outs],
        compiler_params=_cp("parallel"), name="proj")(x2d, *consts)
    return dict(zip(names, res))


def _rope_tables(pos):
    d = jnp.arange(LANES) % HEAD_DIM
    inv_freq = ROPE_THETA ** (-(d % ROT_HALF).astype(F32) * 2.0 / ROT_DIM)
    ang = pos.astype(F32)[:, None] * inv_freq[None, :]
    cos = jnp.where(d < ROT_DIM, jnp.cos(ang), 1.0)
    sa = jnp.where(d < ROT_HALF, -jnp.sin(ang), 0.0)
    sb = jnp.where((d >= ROT_HALF) & (d < ROT_DIM), jnp.sin(ang), 0.0)
    return cos.astype(F32), sa.astype(F32), sb.astype(F32)


def _memkv_kernel(x_ref, g_ref, w_ref, gkm_ref, bd_ref, k_o, v_o):
    h = _rms_rows(x_ref[...], g_ref[...]).astype(BF16)
    wk = N_HEADS_M * HEAD_DIM_M
    y = _dot(h, w_ref[:, :wk])
    ms = _dot((y * y).astype(BF16), bd_ref[...])
    k_o[...] = y * lax.rsqrt(ms + EPS) * gkm_ref[...]
    v_o[...] = _dot(h, w_ref[:, wk:])


def _memkv(mem2d, g_mem, w_kv, gkm4, bd128):
    n = mem2d.shape[0]
    tm = N_MEM
    wk = N_HEADS_M * HEAD_DIM_M
    row = lambda w: pl.BlockSpec((tm, w), lambda i: (i, 0))
    return pl.pallas_call(
        _memkv_kernel, grid=(n // tm,),
        in_specs=[row(D_MODEL), _full_spec(g_mem), _full_spec(w_kv), _full_spec(gkm4), _full_spec(bd128)],
        out_specs=[row(wk), row(wk)],
        out_shape=[jax.ShapeDtypeStruct((n, wk), F32)] * 2,
        compiler_params=_cp("parallel"), name="memkv")(mem2d, g_mem, w_kv, gkm4, bd128)


def _key_to_f32(k):
    bits = k ^ ((k >> 31) & jnp.int32(0x7FFFFFFF))
    return lax.bitcast_convert_type(bits, F32)


def _count(sc_ref, ngrp, rows, unroll, pred):
    def body(g, acc):
        for u in range(unroll):
            off = pl.multiple_of((g * unroll + u) * LANES, LANES)
            acc = acc + pred(sc_ref[:, pl.ds(off, LANES)], off).astype(I32)
        return acc
    acc = lax.fori_loop(0, ngrp, body, jnp.zeros((rows, LANES), I32))
    return jnp.sum(acc, axis=1, keepdims=True)


def _topk_threshold(sc_ref, ngrp, rows, unroll, n_adm, ksel, idx_bits):
    bc = lambda v: jnp.broadcast_to(v, (rows, LANES))

    def count_ge(thr):
        tb = bc(thr)
        return _count(sc_ref, ngrp, rows, unroll, lambda v, off: v >= tb)

    key = jnp.where(count_ge(jnp.zeros((rows, 1), F32)) >= ksel, 0, INT_MIN).astype(I32)

    def bit_step(it, key):
        cand = key | jnp.left_shift(jnp.int32(1), 30 - it)
        return jnp.where(count_ge(_key_to_f32(cand)) >= ksel, cand, key)

    key = lax.fori_loop(0, 31, bit_step, key)
    all_adm = n_adm <= ksel
    thr = jnp.where(all_adm, -FLT_MAX, _key_to_f32(key))
    tb = bc(thr)
    n_gt = _count(sc_ref, ngrp, rows, unroll, lambda v, off: v > tb)
    n_ge = _count(sc_ref, ngrp, rows, unroll, lambda v, off: v >= tb)
    need = ksel - n_gt
    tie = (n_ge > ksel) & jnp.logical_not(all_adm)
    lane = lax.broadcasted_iota(I32, (rows, LANES), 1)

    def tie_search(_):
        def step(it, x):
            cand = x | jnp.left_shift(jnp.int32(1), idx_bits - 1 - it)
            cb = bc(cand)
            cnt = _count(sc_ref, ngrp, rows, unroll, lambda v, off: (v == tb) & ((off + lane) < cb))
            return jnp.where(cnt < need, cand, x)
        return lax.fori_loop(0, idx_bits, step, jnp.zeros((rows, 1), I32))

    any_tie = jnp.max(tie.astype(I32)) > 0
    jcut = lax.cond(any_tie, tie_search, lambda _: jnp.full((rows, 1), BIG_IDX, I32), 0)
    return thr, jnp.where(tie, jcut, BIG_IDX)


def _selected(v, off, tb, jb, lane):
    return (v > tb) | ((v == tb) & ((off + lane) <= jb))


def _sel_prompt_kernel(seq, ksel, qi_ref, wi_ref, ki_ref, o_ref, sc_ref, wb_ref):
    tq, ch = SEL_TQ, SEL_CH
    unroll = ch // LANES
    i = pl.program_id(1)
    ngrp = ((i + 1) * tq + ch - 1) // ch
    w = wi_ref[0]
    for h in range(N_HEADS):
        wb_ref[h] = jnp.broadcast_to(w[:, h:h + 1], (tq, LANES))
    lo = lax.broadcasted_iota(I32, (tq, LANES), 1) < HEAD_DIM
    zero = jnp.zeros((tq, LANES), BF16)
    qh = []
    for pr in range(N_HEADS // 2):
        q2 = qi_ref[0, :, LANES * pr:LANES * (pr + 1)]
        qh += [jnp.where(lo, q2, zero), jnp.where(lo, zero, q2)]
    rowg = i * tq + lax.broadcasted_iota(I32, (tq, ch), 0)
    coll = lax.broadcasted_iota(I32, (tq, ch), 1)

    def chunk(c, carry):
        off = pl.multiple_of(c * ch, ch)
        kc = ki_ref[0, pl.ds(off, ch), :]
        acc = jnp.zeros((tq, ch), F32)
        for h in range(N_HEADS):
            acc = acc + jnp.maximum(_dot_nt(qh[h], kc), 0.0) * jnp.tile(wb_ref[h], (1, unroll))
        sc_ref[:, pl.ds(off, ch)] = jnp.where(off + coll <= rowg, acc, -jnp.inf)
        return carry

    lax.fori_loop(0, ngrp, chunk, 0)
    n_adm = i * tq + lax.broadcasted_iota(I32, (tq, 1), 0) + 1
    thr, jcut = _topk_threshold(sc_ref, ngrp, tq, unroll, n_adm, ksel, (seq - 1).bit_length())
    tb = jnp.broadcast_to(thr, (tq, LANES))
    jb = jnp.broadcast_to(jcut, (tq, LANES))
    lane = lax.broadcasted_iota(I32, (tq, LANES), 1)

    def write(g, carry):
        for u in range(unroll):
            off = pl.multiple_of((g * unroll + u) * LANES, LANES)
            sel = _selected(sc_ref[:, pl.ds(off, LANES)], off, tb, jb, lane)
            o_ref[0, :, pl.ds(off, LANES)] = jnp.where(sel, 0.0, NEG).astype(BF16)
        return carry

    lax.fori_loop(0, ngrp, write, 0)

    def fill(g, carry):
        off = pl.multiple_of(g * ch, ch)
        o_ref[0, :, pl.ds(off, ch)] = jnp.full((tq, ch), NEG, BF16)
        return carry

    lax.fori_loop(ngrp, seq // ch, fill, 0)


def _sel_prompt(qi, wi, ki2):
    b, seq, _ = qi.shape
    tq = SEL_TQ
    ksel = min(TOPK_KEYS, seq // 4)
    return pl.pallas_call(
        functools.partial(_sel_prompt_kernel, seq, ksel), grid=(b, seq // tq),
        in_specs=[pl.BlockSpec((1, tq, W_ATT), lambda bb, i: (bb, i, 0)),
                  pl.BlockSpec((1, tq, N_HEADS), lambda bb, i: (bb, i, 0)),
                  pl.BlockSpec((1, seq, LANES), lambda bb, i: (bb, 0, 0))],
        out_specs=pl.BlockSpec((1, tq, seq), lambda bb, i: (bb, i, 0)),
        out_shape=jax.ShapeDtypeStruct((b, seq, seq), BF16),
        scratch_shapes=[pltpu.VMEM((tq, seq), F32), pltpu.VMEM((N_HEADS, tq, LANES), F32)],
        compiler_params=_cp("parallel", "arbitrary"), name="sel_prompt")(qi, wi, ki2)


def _flash_kernel(mode, q_ref, k_ref, v_ref, b_ref, o_ref, acc_ref, m_ref, l_ref):
    tq, tk = FLASH_TQ, FLASH_TK
    sq, sk = FLASH_SQ, FLASH_SK
    i, j = pl.program_id(1), pl.program_id(2)
    last = ((i + 1) * tq - 1) // tk
    lo = lax.broadcasted_iota(I32, (tq, LANES), 1) < HEAD_DIM

    @pl.when(j == 0)
    def _():
        acc_ref[...] = jnp.zeros(acc_ref.shape, F32)
        m_ref[...] = jnp.full(m_ref.shape, NEG, F32)
        l_ref[...] = jnp.zeros(l_ref.shape, F32)

    def sub_tile(diag, r0, c0):
        zero = jnp.zeros((sq, LANES), BF16)
        los = lax.broadcasted_iota(I32, (sq, LANES), 1) < HEAD_DIM
        rows, cols = pl.ds(r0, sq), pl.ds(c0, sk)
        if mode == "dsa":
            bias = b_ref[0, rows, cols].astype(F32)
        if diag:
            causal = ((j * tk + c0 + lax.broadcasted_iota(I32, (sq, sk), 1))
                      <= (i * tq + r0 + lax.broadcasted_iota(I32, (sq, sk), 0)))
        for pr in range(N_HEADS // 2):
            sl = slice(LANES * pr, LANES * (pr + 1))
            q2, k2, v2 = q_ref[0, rows, sl], k_ref[0, cols, sl], v_ref[0, cols, sl]
            accp = acc_ref[rows, sl]
            new = []
            for hh in range(2):
                h = 2 * pr + hh
                qh = jnp.where(los, q2, zero) if hh == 0 else jnp.where(los, zero, q2)
                s = _dot_nt(qh, k2)
                if mode == "dsa":
                    s = s + bias
                else:
                    s = s + b_ref[0, h:h + 1, cols]
                    if diag:
                        s = jnp.where(causal, s, NEG)
                m_prev = m_ref[h, rows, :]
                m_new = jnp.maximum(m_prev, jnp.max(s, axis=1, keepdims=True))
                alpha = jnp.exp(m_prev - m_new)
                p = jnp.exp(s - jnp.tile(m_new, (1, sk // LANES)))
                l_ref[h, rows, :] = alpha * l_ref[h, rows, :] + jnp.sum(p, axis=1, keepdims=True)
                m_ref[h, rows, :] = m_new
                new.append(alpha * accp + _dot(p.astype(BF16), v2))
            acc_ref[rows, sl] = jnp.where(los, new[0], new[1])

    def step(diag):
        ncb = tk // sk

        def body(t, carry):
            r0 = pl.multiple_of((t // ncb) * sq, sq)
            c0 = pl.multiple_of((t % ncb) * sk, sk)
            if diag:
                @pl.when(j * tk + c0 <= i * tq + r0 + (sq - 1))
                def _():
                    sub_tile(True, r0, c0)
            else:
                sub_tile(False, r0, c0)
            return carry

        lax.fori_loop(0, (tq // sq) * ncb, body, 0)

    def finish():
        for pr in range(N_HEADS // 2):
            sl = slice(LANES * pr, LANES * (pr + 1))
            o_ref[0, :, sl] = (acc_ref[:, sl] / jnp.where(lo, l_ref[2 * pr], l_ref[2 * pr + 1])).astype(o_ref.dtype)

    if mode == "dsa":
        @pl.when(j <= last)
        def _():
            step(False)
    else:
        @pl.when(j < last)
        def _():
            step(False)

        @pl.when(j == last)
        def _():
            step(True)

    @pl.when(j == last)
    def _():
        finish()


def _flash(mode, q, k, v, bias):
    b, seq, _ = q.shape
    tq, tk = FLASH_TQ, FLASH_TK
    assert tk % tq == 0 and seq % tk == 0
    jc = lambda i, j: jnp.minimum(j, ((i + 1) * tq - 1) // tk)
    if mode == "dsa":
        bspec = pl.BlockSpec((1, tq, tk), lambda bb, i, j: (bb, i, jc(i, j)))
    else:
        bspec = pl.BlockSpec((1, N_HEADS, tk), lambda bb, i, j: (bb, 0, jc(i, j)))
    return pl.pallas_call(
        functools.partial(_flash_kernel, mode), grid=(b, seq // tq, seq // tk),
        in_specs=[pl.BlockSpec((1, tq, W_ATT), lambda bb, i, j: (bb, i, 0)),
                  pl.BlockSpec((1, tk, W_ATT), lambda bb, i, j: (bb, jc(i, j), 0)),
                  pl.BlockSpec((1, tk, W_ATT), lambda bb, i, j: (bb, jc(i, j), 0)),
                  bspec],
        out_specs=pl.BlockSpec((1, tq, W_ATT), lambda bb, i, j: (bb, i, 0)),
        out_shape=jax.ShapeDtypeStruct((b, seq, W_ATT), BF16),
        scratch_shapes=[pltpu.VMEM((tq, W_ATT), F32), pltpu.VMEM((N_HEADS, tq, LANES), F32),
                        pltpu.VMEM((N_HEADS, tq, LANES), F32)],
        compiler_params=_cp("parallel", "parallel", "arbitrary"), name="flash_" + mode)(q, k, v, bias)


def _lane_cumsum(x, lane):
    sh = 1
    while sh < LANES:
        x = x + jnp.where(lane >= sh, pltpu.roll(x, sh, 1), 0.0)
        sh *= 2
    return x


def _negcumsum_kernel(seq, x_ref, o_ref):
    lane = lax.broadcasted_iota(I32, (N_HEADS, LANES), 1)

    def body(c, carry):
        off = pl.multiple_of(c * LANES, LANES)
        cs = _lane_cumsum(x_ref[0, :, pl.ds(off, LANES)], lane) + carry
        o_ref[0, :, pl.ds(off, LANES)] = -cs
        return jnp.broadcast_to(cs[:, LANES - 1:LANES], (N_HEADS, LANES))

    lax.fori_loop(0, seq // LANES, body, jnp.zeros((N_HEADS, LANES), F32))


def _negcumsum(logf_t):
    b, _, seq = logf_t.shape
    spec = pl.BlockSpec((1, N_HEADS, seq), lambda bb: (bb, 0, 0))
    return pl.pallas_call(functools.partial(_negcumsum_kernel, seq), grid=(b,), in_specs=[spec], out_specs=spec,
                          out_shape=jax.ShapeDtypeStruct(logf_t.shape, F32),
                          compiler_params=_cp("parallel"), name="negcumsum")(logf_t)


def _memattn_kernel(q_ref, k_ref, v_ref, o_ref):
    for h in range(N_HEADS_M):
        sl = slice(HEAD_DIM_M * h, HEAD_DIM_M * (h + 1))
        kh = k_ref[0, 0, :, h, :].astype(BF16)
        vh = v_ref[0, 0, :, h, :].astype(BF16)
        s = _dot_nt(q_ref[0, :, sl], kh) * (HEAD_DIM_M ** -0.5)
        p = jnp.exp(s - jnp.max(s, axis=1, keepdims=True))
        o = _dot(p.astype(BF16), vh) / jnp.sum(p, axis=1, keepdims=True)
        o_ref[0, :, sl] = o.astype(o_ref.dtype)


def _memattn(q, k5, v5, layer, tq):
    b, t, w = q.shape
    kv = pl.BlockSpec((1, 1, N_MEM, N_HEADS_M, HEAD_DIM_M), lambda bb, i: (layer, bb, 0, 0, 0))
    qs = pl.BlockSpec((1, tq, w), lambda bb, i: (bb, i, 0))
    return pl.pallas_call(_memattn_kernel, grid=(b, t // tq), in_specs=[qs, kv, kv], out_specs=qs,
                          out_shape=jax.ShapeDtypeStruct(q.shape, BF16),
                          compiler_params=_cp("parallel", "parallel"), name="memattn")(q, k5, v5)


def _sel_sample_kernel(nsteps, pps, past, ksel, unroll, pt_ref, qi_ref, wb_ref, *refs):
    page_refs, knew_ref, o_ref, sc_ref = refs[:pps], refs[pps], refs[pps + 1], refs[pps + 2]
    j = pl.program_id(1)
    t_new = qi_ref.shape[1] // N_HEADS
    q = qi_ref[0]
    wb = wb_ref[0]

    def score(kp):
        n = kp.shape[1]
        r = jnp.maximum(_dot(q, kp.astype(BF16)), 0.0) * jnp.tile(wb, (1, n // LANES))
        return jnp.sum(r.reshape(N_HEADS, t_new, n), axis=0)

    off = pl.multiple_of(j * (pps * PAGE), pps * PAGE)
    sc_ref[:, pl.ds(off, pps * PAGE)] = score(jnp.concatenate([pr[0, 0] for pr in page_refs], axis=1))

    @pl.when(j == nsteps - 1)
    def _():
        trow = lax.broadcasted_iota(I32, (t_new, LANES), 0)
        lane = lax.broadcasted_iota(I32, (t_new, LANES), 1)
        sc_ref[:, past:past + LANES] = jnp.where(lane <= trow, score(knew_ref[0]), -jnp.inf)
        total = past + LANES
        n_adm = past + lax.broadcasted_iota(I32, (t_new, 1), 0) + 1
        thr, jcut = _topk_threshold(sc_ref, total // (LANES * unroll), t_new, unroll, n_adm, ksel, (total - 1).bit_length())
        tb = jnp.broadcast_to(thr, (t_new, LANES))
        jb = jnp.broadcast_to(jcut, (t_new, LANES))
        for c in range(total // LANES):
            off = c * LANES
            sel = _selected(sc_ref[:, off:off + LANES], off, tb, jb, lane)
            o_ref[0, :, off:off + LANES] = jnp.where(sel, 0.0, NEG)


def _sel_sample(pt_flat, qi_hs, wb_hs, cache_kidx, layer, knew, npg):
    db = qi_hs.shape[0]
    t_new = qi_hs.shape[1] // N_HEADS
    past = npg * PAGE
    total = past + LANES
    pps = SEL_PPS if npg % SEL_PPS == 0 else 1
    nsteps = npg // pps
    ncol = total // LANES
    unroll = max(u for u in range(1, 9) if ncol % u == 0)
    ksel = min(TOPK_KEYS, (past + t_new) // 4)
    page = lambda r: pl.BlockSpec((1, 1, IDX_DIM, PAGE), lambda b, j, pt: (layer, pt[b * npg + j * pps + r], 0, 0))
    return pl.pallas_call(
        functools.partial(_sel_sample_kernel, nsteps, pps, past, ksel, unroll),
        grid_spec=pltpu.PrefetchScalarGridSpec(
            num_scalar_prefetch=1, grid=(db, nsteps),
            in_specs=[pl.BlockSpec((1, N_HEADS * t_new, IDX_DIM), lambda b, j, pt: (b, 0, 0)),
                      pl.BlockSpec((1, N_HEADS * t_new, LANES), lambda b, j, pt: (b, 0, 0))]
                     + [page(r) for r in range(pps)]
                     + [pl.BlockSpec((1, IDX_DIM, PAGE), lambda b, j, pt: (b, 0, 0))],
            out_specs=pl.BlockSpec((1, t_new, total), lambda b, j, pt: (b, 0, 0)),
            scratch_shapes=[pltpu.VMEM((t_new, total), F32)]),
        out_shape=jax.ShapeDtypeStruct((db, t_new, total), F32),
        compiler_params=_cp("parallel", "arbitrary"), name="sel_sample",
    )(pt_flat, qi_hs, wb_hs, *([cache_kidx] * pps), knew)


def _paged_kernel(mode, nsteps, pps, pt_ref, q_ref, *refs):
    k_refs, v_refs, b_refs = refs[:pps], refs[pps:2 * pps], refs[2 * pps:3 * pps]
    kn_ref, vn_ref, bn_ref, o_ref, acc_ref, m_ref, l_ref, coff_ref = refs[3 * pps:]
    j = pl.program_id(1)
    t_new = o_ref.shape[2]
    rows = N_HEADS * t_new
    lane = lax.broadcasted_iota(I32, (N_HEADS, LANES), 1)

    @pl.when(j == 0)
    def _():
        acc_ref[...] = jnp.zeros(acc_ref.shape, F32)
        m_ref[...] = jnp.full(m_ref.shape, NEG, F32)
        l_ref[...] = jnp.zeros(l_ref.shape, F32)
        coff_ref[...] = jnp.zeros(coff_ref.shape, F32)

    def flat(ref_page):
        return ref_page.reshape(W_ATT, PAGE).astype(BF16)

    def attend(kt, vt, bias):
        n = kt.shape[1]
        s = _dot(q_ref[0], kt) + bias
        m_prev = m_ref[...]
        m_new = jnp.maximum(m_prev, jnp.max(s, axis=1, keepdims=True))
        alpha = jnp.exp(m_prev - m_new)
        p = jnp.exp(s - jnp.tile(m_new, (1, n // LANES)))
        l_ref[...] = alpha * l_ref[...] + jnp.sum(p, axis=1, keepdims=True)
        m_ref[...] = m_new
        acc_ref[...] = jnp.tile(alpha, (1, W_ATT // LANES)) * acc_ref[...] + _dot_nt(p.astype(BF16), vt)

    def neg_cum(lf):
        c = _lane_cumsum(lf, lane) + coff_ref[...]
        coff_ref[...] = jnp.broadcast_to(c[:, LANES - 1:LANES], (N_HEADS, LANES))
        return -c

    def per_head_rows(x):
        return jnp.broadcast_to(x[:, None, :], (N_HEADS, t_new, x.shape[1])).reshape(rows, x.shape[1])

    def per_token_rows(x):
        return jnp.tile(x, (N_HEADS, 1))

    if mode == "fox":
        bias = jnp.concatenate([per_head_rows(neg_cum(br[0, 0])) for br in b_refs], axis=1)
    else:
        bias = jnp.concatenate([per_token_rows(br[0]) for br in b_refs], axis=1)
    attend(jnp.concatenate([flat(kr[0, 0]) for kr in k_refs], axis=1),
           jnp.concatenate([flat(vr[0, 0]) for vr in v_refs], axis=1), bias)

    @pl.when(j == nsteps - 1)
    def _():
        if mode == "fox":
            tok = lax.broadcasted_iota(I32, (N_HEADS, t_new, LANES), 1).reshape(rows, LANES)
            causal = lax.broadcasted_iota(I32, (rows, LANES), 1) <= tok
            bias_new = jnp.where(causal, per_head_rows(neg_cum(bn_ref[0])), NEG)
        else:
            bias_new = per_token_rows(bn_ref[0])
        attend(flat(kn_ref[0]), flat(vn_ref[0]), bias_new)
        for h in range(N_HEADS):
            rs = slice(t_new * h, t_new * (h + 1))
            o_ref[0, h] = (acc_ref[rs, HEAD_DIM * h:HEAD_DIM * (h + 1)] / l_ref[rs, :HEAD_DIM]).astype(o_ref.dtype)


def _block_diag_q(q_hs):
    db, nh, t_new, d = q_hs.shape
    eye = jnp.eye(nh, dtype=q_hs.dtype)
    return (q_hs[:, :, :, None, :] * eye[None, :, None, :, None]).reshape(db, nh * t_new, nh * d)


def _paged(mode, pt_flat, q_hs, cache_k, cache_v, layer, bias_src, knew, vnew, bias_new, npg):
    db, _, t_new, _ = q_hs.shape
    assert t_new == N_HEADS
    pps = ATT_PPS if npg % ATT_PPS == 0 else 1
    nsteps = npg // pps
    rows = N_HEADS * t_new
    pidx = lambda r: (lambda b, j, pt: (layer, pt[b * npg + j * pps + r], 0, 0, 0))
    page = lambda r: pl.BlockSpec((1, 1, N_HEADS, HEAD_DIM, PAGE), pidx(r))
    if mode == "fox":
        bspec = lambda r: pl.BlockSpec((1, 1, N_HEADS, PAGE), lambda b, j, pt: (layer, pt[b * npg + j * pps + r], 0, 0))
        bnew = pl.BlockSpec((1, N_HEADS, LANES), lambda b, j, pt: (b, 0, 0))
    else:
        bspec = lambda r: pl.BlockSpec((1, t_new, PAGE), lambda b, j, pt: (b, 0, j * pps + r))
        bnew = pl.BlockSpec((1, t_new, LANES), lambda b, j, pt: (b, 0, npg))
    newp = pl.BlockSpec((1, N_HEADS, HEAD_DIM, PAGE), lambda b, j, pt: (b, 0, 0, 0))
    qs = pl.BlockSpec((1, rows, W_ATT), lambda b, j, pt: (b, 0, 0))
    return pl.pallas_call(
        functools.partial(_paged_kernel, mode, nsteps, pps),
        grid_spec=pltpu.PrefetchScalarGridSpec(
            num_scalar_prefetch=1, grid=(db, nsteps),
            in_specs=[qs] + [page(r) for r in range(pps)] + [page(r) for r in range(pps)]
                     + [bspec(r) for r in range(pps)] + [newp, newp, bnew],
            out_specs=pl.BlockSpec((1, N_HEADS, t_new, HEAD_DIM), lambda b, j, pt: (b, 0, 0, 0)),
            scratch_shapes=[pltpu.VMEM((rows, W_ATT), F32), pltpu.VMEM((rows, LANES), F32),
                            pltpu.VMEM((rows, LANES), F32), pltpu.VMEM((N_HEADS, LANES), F32)]),
        out_shape=jax.ShapeDtypeStruct(q_hs.shape, BF16),
        compiler_params=_cp("parallel", "arbitrary"), name="paged_" + mode,
    )(pt_flat, _block_diag_q(q_hs), *([cache_k] * pps), *([cache_v] * pps), *([bias_src] * pps), knew, vnew, bias_new)


def _blockout_kernel(x_ref, oa_ref, ob_ref, om_ref, gmix_ref, wg_ref, wpa_ref, wpb_ref, wpm_ref, wo_ref, gffn_ref,
                     wr_ref, br_ref, x1_o, h2_o, te_o, tg_o):
    x = x_ref[...]
    tm = x.shape[0]
    h = _rms_rows(x, gmix_ref[...]).astype(BF16)
    merged = jnp.zeros((tm, D_MODEL), F32)
    for n, (o_ref, wp_ref) in enumerate(((oa_ref, wpa_ref), (ob_ref, wpb_ref), (om_ref, wpm_ref))):
        gate = jax.nn.sigmoid(_dot(h, wg_ref[:, D_MODEL * n:D_MODEL * (n + 1)]))
        merged = merged + gate * _dot(o_ref[...], wp_ref[...])
    x1 = x + _dot(merged.astype(BF16), wo_ref[...])
    x1_o[...] = x1
    h2 = _rms_rows(x1, gffn_ref[...])
    for c in range(D_MODEL // LANES):
        h2_o[:, c, :] = h2[:, LANES * c:LANES * (c + 1)]
    logits = _dot(h2.astype(BF16), wr_ref[...]) + br_ref[...]
    lane = lax.broadcasted_iota(I32, (tm, LANES), 1)
    vals, te = [], jnp.zeros((tm, LANES), I32)
    for k in range(TOP_K):
        m = jnp.max(logits, axis=1, keepdims=True)
        ix = jnp.min(jnp.where(logits == m, lane, LANES), axis=1, keepdims=True)
        vals.append(m)
        te = jnp.where(lane == k, ix, te)
        logits = jnp.where(lane == ix, -jnp.inf, logits)
    ex = [jnp.exp(v - vals[0]) for v in vals]
    den = ex[0] + ex[1] + ex[2] + ex[3]
    tg = jnp.zeros((tm, LANES), F32)
    for k in range(TOP_K):
        tg = jnp.where(lane == k, ex[k] / den, tg)
    te_o[...] = te
    tg_o[...] = tg


def _blockout(x2d, oa, ob, om, wts):
    n = x2d.shape[0]
    tm = OUT_TM
    row = lambda w: pl.BlockSpec((tm, w), lambda i: (i, 0))
    names = ("g_mix", "wg", "wpa", "wpb", "wpm", "wo", "g_ffn", "wr", "br")
    nch = D_MODEL // LANES
    return pl.pallas_call(
        _blockout_kernel, grid=(n // tm,),
        in_specs=[row(D_MODEL), row(W_ATT), row(W_ATT), row(N_HEADS_M * HEAD_DIM_M)] + [_full_spec(wts[k]) for k in names],
        out_specs=[row(D_MODEL), pl.BlockSpec((tm, nch, LANES), lambda i: (i, 0, 0)), row(LANES), row(LANES)],
        out_shape=[jax.ShapeDtypeStruct((n, D_MODEL), F32), jax.ShapeDtypeStruct((n, nch, LANES), F32),
                   jax.ShapeDtypeStruct((n, LANES), I32), jax.ShapeDtypeStruct((n, LANES), F32)],
        compiler_params=_cp("parallel"), name="blockout")(x2d, oa, ob, om, *[wts[k] for k in names])


def _row_copy(src_ref, dst_ref, src_row, dst_row, sem):
    return pltpu.make_async_copy(src_ref.at[src_row], dst_ref.at[dst_row], sem)


def _gather_kernel(idx_ref, src_ref, out_ref, sem):
    def issue(r, c):
        _row_copy(src_ref, out_ref, idx_ref[r], r, sem).start()
        return c

    lax.fori_loop(0, GATHER_R, issue, 0)

    def wait(r, c):
        _row_copy(src_ref, out_ref, 0, r, sem).wait()
        return c

    lax.fori_loop(0, GATHER_R, wait, 0)


def _gather_rows(idx, src):
    n = idx.shape[0]
    return pl.pallas_call(
        _gather_kernel, grid=(n // GATHER_R,),
        in_specs=[pl.BlockSpec((GATHER_R,), lambda i: (i,), memory_space=pltpu.SMEM),
                  pl.BlockSpec(memory_space=pl.ANY)],
        out_specs=pl.BlockSpec((GATHER_R,) + src.shape[1:], lambda i: (i, 0, 0)),
        out_shape=jax.ShapeDtypeStruct((n,) + src.shape[1:], src.dtype),
        scratch_shapes=[pltpu.SemaphoreType.DMA(())],
        compiler_params=_cp("arbitrary"), name="moe_gather")(idx, src)


def _ffn_kernel(be_ref, na_ref, x_ref, wu_ref, bu_ref, wd_ref, bd_ref, y_ref):
    nch = D_MODEL // LANES
    i = pl.program_id(0)

    @pl.when(i < na_ref[0])
    def _():
        x = jnp.concatenate([x_ref[:, c, :] for c in range(nch)], axis=1).astype(BF16)
        u = _dot(x, wu_ref[0]) + bu_ref[0]
        glu = jnp.minimum(u[:, :D_FF], SWIGLU_LIMIT)
        lin = jnp.clip(u[:, D_FF:], -SWIGLU_LIMIT, SWIGLU_LIMIT)
        a = glu * jax.nn.sigmoid(SWIGLU_ALPHA * glu) * (lin + 1.0)
        y = _dot(a.astype(BF16), wd_ref[0]) + bd_ref[0]
        for c in range(nch):
            y_ref[:, c, :] = y[:, LANES * c:LANES * (c + 1)]

    @pl.when(i >= na_ref[0])
    def _():
        y_ref[...] = jnp.zeros(y_ref.shape, F32)


def _ffn(blk_expert, n_active, xs, w_up, b_up, w_down, b_down):
    n = xs.shape[0]
    nch = D_MODEL // LANES
    xspec = pl.BlockSpec((MOE_BLK, nch, LANES), lambda i, be, na: (i, 0, 0))
    ex = lambda shape: pl.BlockSpec((1,) + shape, lambda i, be, na: (be[i], 0, 0))
    return pl.pallas_call(
        _ffn_kernel,
        grid_spec=pltpu.PrefetchScalarGridSpec(
            num_scalar_prefetch=2, grid=(n // MOE_BLK,),
            in_specs=[xspec, ex((D_MODEL, 2 * D_FF)), ex((1, 2 * D_FF)), ex((D_FF, D_MODEL)), ex((1, D_MODEL))],
            out_specs=xspec),
        out_shape=jax.ShapeDtypeStruct(xs.shape, F32),
        compiler_params=_cp("arbitrary"), name="moe_ffn")(blk_expert, n_active, xs, w_up, b_up, w_down, b_down)


def _combine_kernel(idx_ref, ys_ref, x1_ref, g_ref, o_ref, buf_ref, sem):
    tm = COMB_TM
    nch = D_MODEL // LANES

    def issue(r, c):
        _row_copy(ys_ref, buf_ref, idx_ref[r], r, sem).start()
        return c

    lax.fori_loop(0, TOP_K * tm, issue, 0)

    def wait(r, c):
        _row_copy(ys_ref, buf_ref, 0, r, sem).wait()
        return c

    lax.fori_loop(0, TOP_K * tm, wait, 0)
    g = g_ref[...]
    y = x1_ref[...]
    for k in range(TOP_K):
        rows = jnp.concatenate([buf_ref[k * tm:(k + 1) * tm, c, :] for c in range(nch)], axis=1)
        y = y + g[:, k:k + 1] * rows
    o_ref[...] = y


def _combine(dest_tiles, ys, x1, gates):
    n = x1.shape[0]
    tm = COMB_TM
    row = lambda w: pl.BlockSpec((tm, w), lambda i: (i, 0))
    return pl.pallas_call(
        _combine_kernel, grid=(n // tm,),
        in_specs=[pl.BlockSpec((TOP_K * tm,), lambda i: (i,), memory_space=pltpu.SMEM),
                  pl.BlockSpec(memory_space=pl.ANY), row(D_MODEL), row(LANES)],
        out_specs=row(D_MODEL),
        out_shape=jax.ShapeDtypeStruct((n, D_MODEL), F32),
        scratch_shapes=[pltpu.VMEM((TOP_K * tm,) + ys.shape[1:], F32), pltpu.SemaphoreType.DMA(())],
        compiler_params=_cp("arbitrary"), name="moe_combine")(dest_tiles, ys, x1, gates)


def _moe(x1, h2, top_e, top_g, wts):
    n = x1.shape[0]
    n_assign = n * TOP_K
    flat_e = top_e[:, :TOP_K].reshape(-1)
    onehot = (flat_e[:, None] == jnp.arange(N_EXPERTS, dtype=I32)[None, :]).astype(I32)
    csum = jnp.cumsum(onehot, axis=0)
    rank = jnp.take_along_axis(csum, flat_e[:, None], axis=1)[:, 0] - 1
    counts = csum[-1]
    padded = (counts + MOE_BLK - 1) // MOE_BLK * MOE_BLK
    pad_end = jnp.cumsum(padded)
    dest = (pad_end - padded)[flat_e] + rank
    n_blocks = -(-n_assign // MOE_BLK) + N_EXPERTS
    n_rows = -(-(n_blocks * MOE_BLK) // GATHER_R) * GATHER_R
    row_tok = jnp.zeros((n_rows,), I32).at[dest].set(jnp.arange(n_assign, dtype=I32) // TOP_K)
    nb = n_rows // MOE_BLK
    blk_expert = jnp.minimum(jnp.searchsorted(pad_end, jnp.arange(nb, dtype=I32) * MOE_BLK, side="right"),
                             N_EXPERTS - 1).astype(I32)
    n_active = (pad_end[-1:] // MOE_BLK).astype(I32)
    xs = _gather_rows(row_tok, h2)
    ys = _ffn(blk_expert, n_active, xs, wts["w_up"], wts["b_up"], wts["w_down"], wts["b_down"])
    dest_tiles = dest.reshape(n // COMB_TM, COMB_TM, TOP_K).transpose(0, 2, 1).reshape(-1)
    return _combine(dest_tiles, ys, x1, top_g)


def _block_diag(width, blk):
    r = jnp.arange(width) // blk
    return jnp.where(r[:, None] == r[None, :], 1.0 / blk, 0.0).astype(BF16)


def _pad_cols(a, width):
    return jnp.pad(a, ((0, 0), (0, width - a.shape[1])))


def _layer_weights(l, w_in, b_forget, g_mix, g_qa, g_ka, g_kidx, g_qb, g_kb, g_qm, g_mem, w_mem_kv, g_km, w_pa, w_pb,
                   w_pm, w_o, g_ffn, w_router, b_router, w_up, b_up, w_down, b_down):
    pts = [0]
    for s in SPLIT_SIZES:
        pts.append(pts[-1] + s)
    col = lambda n: w_in[l][:, pts[n]:pts[n + 1]]
    qa, ka, va, qi, ki, wi, qb, kb, vb, fb, qm, ga, gb, gm = [col(n) for n in range(len(SPLIT_SIZES))]
    rowv = lambda v: v.astype(F32)[None, :]
    scale = HEAD_DIM ** -0.5
    return dict(
        wbig=jnp.concatenate([qa, ka, va, qi, qb, kb, vb, qm], axis=1).astype(BF16),
        wsm=jnp.concatenate([_pad_cols(ki, LANES), _pad_cols(wi, LANES), _pad_cols(fb, LANES)], axis=1).astype(BF16),
        wg=jnp.concatenate([ga, gb, gm], axis=1).astype(BF16),
        g_mix=rowv(g_mix[l]), g_ffn=rowv(g_ffn[l]), g_mem=rowv(g_mem[l]),
        gqa=rowv(jnp.tile(g_qa[l] * scale, N_HEADS)), gka=rowv(jnp.tile(g_ka[l], N_HEADS)),
        gqb=rowv(jnp.tile(g_qb[l] * scale, N_HEADS)), gkb=rowv(jnp.tile(g_kb[l], N_HEADS)),
        gqm=rowv(jnp.tile(g_qm[l], N_HEADS_M)), gkm=rowv(jnp.tile(g_km[l], N_HEADS_M)),
        gki=_pad_cols(rowv(g_kidx[l]), LANES), bf=_pad_cols(rowv(b_forget[l]), LANES),
        bd64=_block_diag(W_ATT, HEAD_DIM), bd128=_block_diag(N_HEADS_M * HEAD_DIM_M, HEAD_DIM_M),
        w_mem_kv=w_mem_kv[l].astype(BF16),
        wpa=w_pa[l].astype(BF16), wpb=w_pb[l].astype(BF16), wpm=w_pm[l].astype(BF16), wo=w_o[l].astype(BF16),
        wr=_pad_cols(w_router[l], LANES).astype(BF16),
        br=jnp.pad(rowv(b_router[l]), ((0, 0), (0, LANES - N_EXPERTS)), constant_values=NEG),
        w_up=w_up[l].astype(BF16), b_up=b_up[l].astype(F32)[:, None, :],
        w_down=w_down[l].astype(BF16), b_down=b_down[l].astype(F32)[:, None, :],
    )


def _finish_block(x2d, oa, ob, om, wts):
    x1, h2, te, tg = _blockout(x2d, oa, ob, om, wts)
    return _moe(x1, h2, te, tg, wts)


def kernel(x_prompt, x_sample, mem_prompt, cache_a_k, cache_a_v, cache_a_kidx, cache_b_k, cache_b_v, cache_b_logf,
           cache_mem_k, cache_mem_v, page_table, w_in, b_forget, g_mix, g_qa, g_ka, g_kidx, g_qb, g_kb, g_qm, g_mem,
           w_mem_kv, g_km, w_pa, w_pb, w_pm, w_o, g_ffn, w_router, b_router, w_up, b_up, w_down, b_down):
    depth = w_in.shape[0]
    b, seq, _ = x_prompt.shape
    db, t_new, _ = x_sample.shape
    npg = page_table.shape[1]
    past = npg * PAGE
    pt_flat = page_table.reshape(-1).astype(I32)
    tabs_p = _rope_tables(jnp.arange(seq))
    tabs_s = _rope_tables(past + jnp.arange(PROJ_TM) % t_new)
    page_t = lambda c: jnp.transpose(c, (0, 1, 3, 4, 2))
    cak_t, cav_t, cbk_t, cbv_t = page_t(cache_a_k), page_t(cache_a_v), page_t(cache_b_k), page_t(cache_b_v)
    kidx_t = jnp.swapaxes(cache_a_kidx, 2, 3)
    logf_cache_t = jnp.swapaxes(cache_b_logf, 2, 3)
    hp = x_prompt.reshape(b * seq, D_MODEL)
    hs = x_sample.reshape(db * t_new, D_MODEL)
    p_states, s_states = [], []
    heads = lambda t, nb, nt: t.reshape(nb, nt, N_HEADS, HEAD_DIM)
    for l in range(depth):
        wts = _layer_weights(l, w_in, b_forget, g_mix, g_qa, g_ka, g_kidx, g_qb, g_kb, g_qm, g_mem, w_mem_kv, g_km,
                             w_pa, w_pb, w_pm, w_o, g_ffn, w_router, b_router, w_up, b_up, w_down, b_down)
        pr = _proj(hp, tabs_p, seq // PROJ_TM, wts)
        mk, mv = _memkv(mem_prompt.reshape(b * N_MEM, D_MODEL), wts["g_mem"], wts["w_mem_kv"], wts["gkm"], wts["bd128"])
        mk5 = mk.reshape(1, b, N_MEM, N_HEADS_M, HEAD_DIM_M)
        mv5 = mv.reshape(1, b, N_MEM, N_HEADS_M, HEAD_DIM_M)
        r3 = lambda t: t.reshape(b, seq, t.shape[-1])
        sel_bias = _sel_prompt(r3(pr["qi"]), r3(pr["wi"]), r3(pr["ki2"]))
        o_a = _flash("dsa", r3(pr["qa"]), r3(pr["ka_bf"]), r3(pr["va_bf"]), sel_bias)
        negc = _negcumsum(jnp.swapaxes(r3(pr["logf"]), 1, 2))
        o_b = _flash("fox", r3(pr["qb"]), r3(pr["kb_bf"]), r3(pr["vb_bf"]), negc)
        o_m = _memattn(r3(pr["qm"]), mk5, mv5, 0, MEM_TQ)
        rf = lambda t: t.reshape(b * seq, t.shape[-1])
        p_states.append((heads(pr["ka"], b, seq), heads(pr["va"], b, seq), r3(pr["ki"]), heads(pr["kb"], b, seq),
                         heads(pr["vb"], b, seq), r3(pr["logf"]), mk5[0], mv5[0]))
        hp = _finish_block(hp, rf(o_a), rf(o_b), rf(o_m), wts)
        sr = _proj(hs, tabs_s, 1, wts)
        hs_t = lambda t, d: jnp.swapaxes(t.reshape(db, t_new, N_HEADS, d), 1, 2)
        new_page = lambda t: jnp.pad(jnp.transpose(heads(t, db, t_new), (0, 2, 3, 1)),
                                     ((0, 0), (0, 0), (0, 0), (0, PAGE - t_new)))
        qi_hs = hs_t(sr["qi"], IDX_DIM).reshape(db, N_HEADS * t_new, IDX_DIM)
        wb_hs = jnp.broadcast_to(jnp.swapaxes(sr["wi"].reshape(db, t_new, N_HEADS), 1, 2).reshape(db, N_HEADS * t_new, 1),
                                 (db, N_HEADS * t_new, LANES))
        ki_new = jnp.pad(jnp.swapaxes(sr["ki"].reshape(db, t_new, IDX_DIM), 1, 2), ((0, 0), (0, 0), (0, PAGE - t_new)))
        sel_s = _sel_sample(pt_flat, qi_hs, wb_hs, kidx_t, l, ki_new, npg)
        o_a = _paged("dsa", pt_flat, hs_t(sr["qa"], HEAD_DIM), cak_t, cav_t, l, sel_s,
                     new_page(sr["ka"]), new_page(sr["va"]), sel_s, npg)
        lf_new = jnp.pad(jnp.swapaxes(sr["logf"].reshape(db, t_new, N_HEADS), 1, 2), ((0, 0), (0, 0), (0, LANES - t_new)))
        o_b = _paged("fox", pt_flat, hs_t(sr["qb"], HEAD_DIM), cbk_t, cbv_t, l, logf_cache_t,
                     new_page(sr["kb"]), new_page(sr["vb"]), lf_new, npg)
        o_m = _memattn(sr["qm"].reshape(db, t_new, -1), cache_mem_k, cache_mem_v, l, t_new)
        un_hs = lambda o: jnp.swapaxes(o, 1, 2).reshape(db * t_new, W_ATT)
        s_states.append((heads(sr["ka"], db, t_new), heads(sr["va"], db, t_new), sr["ki"].reshape(db, t_new, IDX_DIM),
                         heads(sr["kb"], db, t_new), heads(sr["vb"], db, t_new), sr["logf"].reshape(db, t_new, N_HEADS)))
        hs = _finish_block(hs, un_hs(o_a), un_hs(o_b), o_m.reshape(db * t_new, -1), wts)
    stack = lambda states: [jnp.stack(s, 0) for s in zip(*states)]
    return (hp.reshape(b, seq, D_MODEL), hs.reshape(db, t_new, D_MODEL), *stack(p_states), *stack(s_states))
```

```python
import functools

import jax
import jax.numpy as jnp
from jax import lax
from jax.experimental import pallas as pl
from jax.experimental.pallas import tpu as pltpu

F32, BF16, I32 = jnp.float32, jnp.bfloat16, jnp.int32

D_MODEL = 1024
HEAD_DIM = 64
N_HEADS = 8
W_ATT = N_HEADS * HEAD_DIM
N_HEADS_M = 4
HEAD_DIM_M = 128
IDX_DIM = 64
TOPK_KEYS = 256
N_MEM = 256
ROPE_THETA = 500000.0
ROT_DIM = HEAD_DIM // 4
ROT_HALF = ROT_DIM // 2
N_EXPERTS = 32
TOP_K = 4
D_FF = 1024
SWIGLU_LIMIT = 7.0
SWIGLU_ALPHA = 1.702
PAGE = 128
EPS = 1e-6
IDX_SCALE = (N_HEADS * IDX_DIM) ** -0.5
SPLIT_SIZES = (W_ATT, W_ATT, W_ATT, N_HEADS * IDX_DIM, IDX_DIM, N_HEADS,
               W_ATT, W_ATT, W_ATT, N_HEADS, N_HEADS_M * HEAD_DIM_M, D_MODEL, D_MODEL, D_MODEL)

LANES = 128
NEG = -1e30
FLT_MAX = 3.4028234663852886e38
LOG2E = 1.4426950408889634
INT_MIN = -2147483648
BIG_IDX = 1 << 30
VMEM_LIMIT = 56 * 1024 * 1024

PROJ_TM = 256
FLASH_TQ, FLASH_TK = 1024, 1024
FLASH_SQ, FLASH_SK = 1024, 1024
SEL_TQ, SEL_CH = 128, 512
MEM_TQ = 512
OUT_TM = 256
MOE_BLK = 256
GATHER_R = 512
COMB_TM = 256
SEL_PPS = 8
ATT_PPS = 8


def _cp(*sem):
    return pltpu.CompilerParams(dimension_semantics=sem, vmem_limit_bytes=VMEM_LIMIT)


def _dot(a, b):
    return jnp.dot(a, b, preferred_element_type=F32)


def _dot_nt(a, b):
    return lax.dot_general(a, b, (((1,), (1,)), ((), ())), preferred_element_type=F32)


def _full_spec(a):
    nd = a.ndim
    return pl.BlockSpec(a.shape, lambda *_: (0,) * nd)


def _rms_rows(x, g):
    return x * lax.rsqrt(jnp.mean(x * x, axis=-1, keepdims=True) + EPS) * g


def _proj_kernel(x_ref, gmix_ref, wbig_ref, wsm_ref, cos_ref, sa_ref, sb_ref, gqa_ref, gka_ref, gqb_ref, gkb_ref,
                 gqm_ref, gki_ref, bf_ref, bd64_ref, bd128_ref,
                 qa_o, ka_o, kab_o, va_o, vab_o, qi_o, ki_o, ki2_o, wi_o, qb_o, kb_o, kbb_o, vb_o, vbb_o, lf_o, qm_o):
    h = _rms_rows(x_ref[...], gmix_ref[...]).astype(BF16)
    cos, sa, sb = cos_ref[...], sa_ref[...], sb_ref[...]

    def seg(j):
        return _dot(h, wbig_ref[:, W_ATT * j:W_ATT * (j + 1)])

    def hnorm(y, bd_ref, g_ref):
        ms = _dot((y * y).astype(BF16), bd_ref[...])
        return y * lax.rsqrt(ms + EPS) * g_ref[...]

    def rope(y):
        parts = []
        for j in range(y.shape[1] // LANES):
            yc = y[:, LANES * j:LANES * (j + 1)]
            parts.append(yc * cos + pltpu.roll(yc, LANES - ROT_HALF, 1) * sa + pltpu.roll(yc, ROT_HALF, 1) * sb)
        return parts[0] if len(parts) == 1 else jnp.concatenate(parts, axis=1)

    qa_o[...] = rope(hnorm(seg(0), bd64_ref, gqa_ref)).astype(BF16)
    ka = rope(hnorm(seg(1), bd64_ref, gka_ref))
    ka_o[...] = ka
    kab_o[...] = ka.astype(BF16)
    va = seg(2)
    va_o[...] = va
    vab_o[...] = va.astype(BF16)
    qi_o[...] = rope(seg(3)).astype(BF16)
    qb_o[...] = hnorm(seg(4), bd64_ref, gqb_ref).astype(BF16)
    kb = hnorm(seg(5), bd64_ref, gkb_ref)
    kb_o[...] = kb
    kbb_o[...] = kb.astype(BF16)
    vb = seg(6)
    vb_o[...] = vb
    vbb_o[...] = vb.astype(BF16)
    qm_o[...] = hnorm(seg(7), bd128_ref, gqm_ref).astype(BF16)

    ys = _dot(h, wsm_ref[:, 0:LANES])
    ms = jnp.sum(ys * ys, axis=-1, keepdims=True) * (1.0 / IDX_DIM)
    ki = rope(ys * lax.rsqrt(ms + EPS) * gki_ref[...])
    ki_o[...] = ki[:, :IDX_DIM]
    ki2_o[...] = (ki + pltpu.roll(ki, IDX_DIM, 1)).astype(BF16)
    wi_o[...] = _dot(h, wsm_ref[:, LANES:2 * LANES])[:, :N_HEADS] * IDX_SCALE
    z = _dot(h, wsm_ref[:, 2 * LANES:3 * LANES]) + bf_ref[...]
    lf_o[...] = (jnp.minimum(z, 0.0) - jnp.log1p(jnp.exp(-jnp.abs(z))))[:, :N_HEADS]


def _proj(x2d, tabs, nper, wts):
    n = x2d.shape[0]
    tm = PROJ_TM
    row = lambda w: pl.BlockSpec((tm, w), lambda i: (i, 0))
    tab = pl.BlockSpec((tm, LANES), lambda i: (i % nper, 0))
    outs = [(W_ATT, BF16), (W_ATT, F32), (W_ATT, BF16), (W_ATT, F32), (W_ATT, BF16), (W_ATT, BF16), (IDX_DIM, F32),
            (LANES, BF16), (N_HEADS, F32), (W_ATT, BF16), (W_ATT, F32), (W_ATT, BF16), (W_ATT, F32), (W_ATT, BF16),
            (N_HEADS, F32), (W_ATT, BF16)]
    consts = [wts[k] for k in ("g_mix", "wbig", "wsm")] + list(tabs) + [
        wts[k] for k in ("gqa", "gka", "gqb", "gkb", "gqm", "gki", "bf", "bd64", "bd128")]
    in_specs = [row(D_MODEL)] + [_full_spec(wts["g_mix"]), _full_spec(wts["wbig"]), _full_spec(wts["wsm"]), tab, tab, tab] + [
        _full_spec(wts[k]) for k in ("gqa", "gka", "gqb", "gkb", "gqm", "gki", "bf", "bd64", "bd128")]
    names = ("qa", "ka", "ka_bf", "va", "va_bf", "qi", "ki", "ki2", "wi", "qb", "kb", "kb_bf", "vb", "vb_bf", "logf", "qm")
    res = pl.pallas_call(
        _proj_kernel, grid=(n // tm,), in_specs=in_specs,
        out_specs=[row(w) for w, _ in outs],
        out_shape=[jax.ShapeDtypeStruct((n, w), dt) for w, dt in outs],
        compiler_params=_cp("parallel"), name="proj")(x2d, *consts)
    return dict(zip(names, res))


def _rope_tables(pos):
    d = jnp.arange(LANES) % HEAD_DIM
    inv_freq = ROPE_THETA ** (-(d % ROT_HALF).astype(F32) * 2.0 / ROT_DIM)
    ang = pos.astype(F32)[:, None] * inv_freq[None, :]
    cos = jnp.where(d < ROT_DIM, jnp.cos(ang), 1.0)
    sa = jnp.where(d < ROT_HALF, -jnp.sin(ang), 0.0)
    sb = jnp.where((d >= ROT_HALF) & (d < ROT_DIM), jnp.sin(ang), 0.0)
    return cos.astype(F32), sa.astype(F32), sb.astype(F32)


def _memkv_kernel(x_ref, g_ref, w_ref, gkm_ref, bd_ref, k_o, v_o):
    h = _rms_rows(x_ref[...], g_ref[...]).astype(BF16)
    wk = N_HEADS_M * HEAD_DIM_M
    y = _dot(h, w_ref[:, :wk])
    ms = _dot((y * y).astype(BF16), bd_ref[...])
    k_o[...] = y * lax.rsqrt(ms + EPS) * gkm_ref[...]
    v_o[...] = _dot(h, w_ref[:, wk:])


def _memkv(mem2d, g_mem, w_kv, gkm4, bd128):
    n = mem2d.shape[0]
    tm = N_MEM
    wk = N_HEADS_M * HEAD_DIM_M
    row = lambda w: pl.BlockSpec((tm, w), lambda i: (i, 0))
    return pl.pallas_call(
        _memkv_kernel, grid=(n // tm,),
        in_specs=[row(D_MODEL), _full_spec(g_mem), _full_spec(w_kv), _full_spec(gkm4), _full_spec(bd128)],
        out_specs=[row(wk), row(wk)],
        out_shape=[jax.ShapeDtypeStruct((n, wk), F32)] * 2,
        compiler_params=_cp("parallel"), name="memkv")(mem2d, g_mem, w_kv, gkm4, bd128)


def _key_to_f32(k):
    bits = k ^ ((k >> 31) & jnp.int32(0x7FFFFFFF))
    return lax.bitcast_convert_type(bits, F32)


def _count(sc_ref, ngrp, rows, unroll, pred):
    def body(g, acc):
        for u in range(unroll):
            off = pl.multiple_of((g * unroll + u) * LANES, LANES)
            acc = acc + pred(sc_ref[:, pl.ds(off, LANES)], off).astype(I32)
        return acc
    acc = lax.fori_loop(0, ngrp, body, jnp.zeros((rows, LANES), I32))
    return jnp.sum(acc, axis=1, keepdims=True)


def _topk_threshold(sc_ref, ngrp, rows, unroll, n_adm, ksel, idx_bits):
    bc = lambda v: jnp.broadcast_to(v, (rows, LANES))

    def count_ge(thr):
        tb = bc(thr)
        return _count(sc_ref, ngrp, rows, unroll, lambda v, off: v >= tb)

    key = jnp.where(count_ge(jnp.zeros((rows, 1), F32)) >= ksel, 0, INT_MIN).astype(I32)

    def bit_step(it, key):
        cand = key | jnp.left_shift(jnp.int32(1), 30 - it)
        return jnp.where(count_ge(_key_to_f32(cand)) >= ksel, cand, key)

    key = lax.fori_loop(0, 31, bit_step, key)
    all_adm = n_adm <= ksel
    thr = jnp.where(all_adm, -FLT_MAX, _key_to_f32(key))
    tb = bc(thr)
    n_gt = _count(sc_ref, ngrp, rows, unroll, lambda v, off: v > tb)
    n_ge = _count(sc_ref, ngrp, rows, unroll, lambda v, off: v >= tb)
    need = ksel - n_gt
    tie = (n_ge > ksel) & jnp.logical_not(all_adm)
    lane = lax.broadcasted_iota(I32, (rows, LANES), 1)

    def tie_search(_):
        def step(it, x):
            cand = x | jnp.left_shift(jnp.int32(1), idx_bits - 1 - it)
            cb = bc(cand)
            cnt = _count(sc_ref, ngrp, rows, unroll, lambda v, off: (v == tb) & ((off + lane) < cb))
            return jnp.where(cnt < need, cand, x)
        return lax.fori_loop(0, idx_bits, step, jnp.zeros((rows, 1), I32))

    any_tie = jnp.max(tie.astype(I32)) > 0
    jcut = lax.cond(any_tie, tie_search, lambda _: jnp.full((rows, 1), BIG_IDX, I32), 0)
    return tb, bc(jnp.where(tie, jcut, BIG_IDX))


def _selected(v, off, tb, jb, lane):
    return (v > tb) | ((v == tb) & ((off + lane) <= jb))


def _sel_prompt_kernel(seq, ksel, qi_ref, wi_ref, ki_ref, o_ref, sc_ref, wb_ref):
    tq, ch = SEL_TQ, SEL_CH
    unroll = ch // LANES
    i = pl.program_id(1)
    ngrp = ((i + 1) * tq + ch - 1) // ch
    w = wi_ref[0]
    for h in range(N_HEADS):
        wb_ref[h] = jnp.broadcast_to(w[:, h:h + 1], (tq, LANES))
    lo = lax.broadcasted_iota(I32, (tq, LANES), 1) < HEAD_DIM
    zero = jnp.zeros((tq, LANES), BF16)
    qh = []
    for pr in range(N_HEADS // 2):
        q2 = qi_ref[0, :, LANES * pr:LANES * (pr + 1)]
        qh += [jnp.where(lo, q2, zero), jnp.where(lo, zero, q2)]
    rowg = i * tq + lax.broadcasted_iota(I32, (tq, ch), 0)
    coll = lax.broadcasted_iota(I32, (tq, ch), 1)

    def chunk(c, carry):
        off = pl.multiple_of(c * ch, ch)
        kc = ki_ref[0, pl.ds(off, ch), :]
        acc = jnp.zeros((tq, ch), F32)
        for h in range(N_HEADS):
            acc = acc + jnp.maximum(_dot_nt(qh[h], kc), 0.0) * jnp.tile(wb_ref[h], (1, unroll))
        sc_ref[:, pl.ds(off, ch)] = jnp.where(off + coll <= rowg, acc, -jnp.inf)
        return carry

    lax.fori_loop(0, ngrp, chunk, 0)
    n_adm = i * tq + lax.broadcasted_iota(I32, (tq, 1), 0) + 1
    tb, jb = _topk_threshold(sc_ref, ngrp, tq, unroll, n_adm, ksel, (seq - 1).bit_length())
    lane = lax.broadcasted_iota(I32, (tq, LANES), 1)

    def write(g, carry):
        for u in range(unroll):
            off = pl.multiple_of((g * unroll + u) * LANES, LANES)
            sel = _selected(sc_ref[:, pl.ds(off, LANES)], off, tb, jb, lane)
            o_ref[0, :, pl.ds(off, LANES)] = jnp.where(sel, 0.0, NEG).astype(BF16)
        return carry

    lax.fori_loop(0, ngrp, write, 0)

    def fill(g, carry):
        off = pl.multiple_of(g * ch, ch)
        o_ref[0, :, pl.ds(off, ch)] = jnp.full((tq, ch), NEG, BF16)
        return carry

    lax.fori_loop(ngrp, seq // ch, fill, 0)


def _sel_prompt(qi, wi, ki2):
    b, seq, _ = qi.shape
    tq = SEL_TQ
    ksel = min(TOPK_KEYS, seq // 4)
    return pl.pallas_call(
        functools.partial(_sel_prompt_kernel, seq, ksel), grid=(b, seq // tq),
        in_specs=[pl.BlockSpec((1, tq, W_ATT), lambda bb, i: (bb, i, 0)),
                  pl.BlockSpec((1, tq, N_HEADS), lambda bb, i: (bb, i, 0)),
                  pl.BlockSpec((1, seq, LANES), lambda bb, i: (bb, 0, 0))],
        out_specs=pl.BlockSpec((1, tq, seq), lambda bb, i: (bb, i, 0)),
        out_shape=jax.ShapeDtypeStruct((b, seq, seq), BF16),
        scratch_shapes=[pltpu.VMEM((tq, seq), F32), pltpu.VMEM((N_HEADS, tq, LANES), F32)],
        compiler_params=_cp("parallel", "arbitrary"), name="sel_prompt")(qi, wi, ki2)


def _flash_kernel(mode, q_ref, k_ref, v_ref, b_ref, o_ref, acc_ref, m_ref, l_ref):
    tq, tk = FLASH_TQ, FLASH_TK
    sq, sk = FLASH_SQ, FLASH_SK
    i, j = pl.program_id(1), pl.program_id(2)
    last = ((i + 1) * tq - 1) // tk
    lo = lax.broadcasted_iota(I32, (tq, LANES), 1) < HEAD_DIM

    @pl.when(j == 0)
    def _():
        acc_ref[...] = jnp.zeros(acc_ref.shape, F32)
        m_ref[...] = jnp.full(m_ref.shape, NEG, F32)
        l_ref[...] = jnp.zeros(l_ref.shape, F32)

    def sub_tile(diag, r0, c0):
        zero = jnp.zeros((sq, LANES), BF16)
        los = lax.broadcasted_iota(I32, (sq, LANES), 1) < HEAD_DIM
        rows, cols = pl.ds(r0, sq), pl.ds(c0, sk)
        if mode == "dsa":
            bias = b_ref[0, rows, cols].astype(F32)
        if diag:
            causal = ((j * tk + c0 + lax.broadcasted_iota(I32, (sq, sk), 1))
                      <= (i * tq + r0 + lax.broadcasted_iota(I32, (sq, sk), 0)))
        for pr in range(N_HEADS // 2):
            sl = slice(LANES * pr, LANES * (pr + 1))
            q2, k2, v2 = q_ref[0, rows, sl], k_ref[0, cols, sl], v_ref[0, cols, sl]
            accp = acc_ref[rows, sl]
            new = []
            for hh in range(2):
                h = 2 * pr + hh
                qh = jnp.where(los, q2, zero) if hh == 0 else jnp.where(los, zero, q2)
                s = _dot_nt(qh, k2)
                if mode == "dsa":
                    s = s + bias
                else:
                    s = s + b_ref[0, h:h + 1, cols]
                    if diag:
                        s = jnp.where(causal, s, NEG)
                m_prev = m_ref[h, rows, :]
                m_new = jnp.maximum(m_prev, jnp.max(s, axis=1, keepdims=True))
                alpha = jnp.exp2(m_prev - m_new)
                p = jnp.exp2(s - jnp.tile(m_new, (1, sk // LANES)))
                l_ref[h, rows, :] = alpha * l_ref[h, rows, :] + jnp.sum(p, axis=1, keepdims=True)
                m_ref[h, rows, :] = m_new
                new.append(alpha * accp + _dot(p.astype(BF16), v2))
            acc_ref[rows, sl] = jnp.where(los, new[0], new[1])

    def step(diag):
        ncb = tk // sk

        def body(t, carry):
            r0 = pl.multiple_of((t // ncb) * sq, sq)
            c0 = pl.multiple_of((t % ncb) * sk, sk)
            if diag:
                @pl.when(j * tk + c0 <= i * tq + r0 + (sq - 1))
                def _():
                    sub_tile(True, r0, c0)
            else:
                sub_tile(False, r0, c0)
            return carry

        lax.fori_loop(0, (tq // sq) * ncb, body, 0)

    def finish():
        for pr in range(N_HEADS // 2):
            sl = slice(LANES * pr, LANES * (pr + 1))
            o_ref[0, :, sl] = (acc_ref[:, sl] / jnp.where(lo, l_ref[2 * pr], l_ref[2 * pr + 1])).astype(o_ref.dtype)

    if mode == "dsa":
        @pl.when(j <= last)
        def _():
            step(False)
    else:
        @pl.when(j < last)
        def _():
            step(False)

        @pl.when(j == last)
        def _():
            step(True)

    @pl.when(j == last)
    def _():
        finish()


def _flash(mode, q, k, v, bias):
    b, seq, _ = q.shape
    tq, tk = FLASH_TQ, FLASH_TK
    assert tk % tq == 0 and seq % tk == 0
    jc = lambda i, j: jnp.minimum(j, ((i + 1) * tq - 1) // tk)
    if mode == "dsa":
        bspec = pl.BlockSpec((1, tq, tk), lambda bb, i, j: (bb, i, jc(i, j)))
    else:
        bspec = pl.BlockSpec((1, N_HEADS, tk), lambda bb, i, j: (bb, 0, jc(i, j)))
    return pl.pallas_call(
        functools.partial(_flash_kernel, mode), grid=(b, seq // tq, seq // tk),
        in_specs=[pl.BlockSpec((1, tq, W_ATT), lambda bb, i, j: (bb, i, 0)),
                  pl.BlockSpec((1, tk, W_ATT), lambda bb, i, j: (bb, jc(i, j), 0)),
                  pl.BlockSpec((1, tk, W_ATT), lambda bb, i, j: (bb, jc(i, j), 0)),
                  bspec],
        out_specs=pl.BlockSpec((1, tq, W_ATT), lambda bb, i, j: (bb, i, 0)),
        out_shape=jax.ShapeDtypeStruct((b, seq, W_ATT), BF16),
        scratch_shapes=[pltpu.VMEM((tq, W_ATT), F32), pltpu.VMEM((N_HEADS, tq, LANES), F32),
                        pltpu.VMEM((N_HEADS, tq, LANES), F32)],
        compiler_params=_cp("parallel", "parallel", "arbitrary"), name="flash_" + mode)(q, k, v, bias)


def _lane_cumsum(x, lane):
    sh = 1
    while sh < LANES:
        x = x + jnp.where(lane >= sh, pltpu.roll(x, sh, 1), 0.0)
        sh *= 2
    return x


def _negcumsum_kernel(seq, x_ref, o_ref):
    lane = lax.broadcasted_iota(I32, (N_HEADS, LANES), 1)

    def body(c, carry):
        off = pl.multiple_of(c * LANES, LANES)
        cs = _lane_cumsum(x_ref[0, :, pl.ds(off, LANES)], lane) + carry
        o_ref[0, :, pl.ds(off, LANES)] = cs * -LOG2E
        return jnp.broadcast_to(cs[:, LANES - 1:LANES], (N_HEADS, LANES))

    lax.fori_loop(0, seq // LANES, body, jnp.zeros((N_HEADS, LANES), F32))


def _negcumsum(logf_t):
    b, _, seq = logf_t.shape
    spec = pl.BlockSpec((1, N_HEADS, seq), lambda bb: (bb, 0, 0))
    return pl.pallas_call(functools.partial(_negcumsum_kernel, seq), grid=(b,), in_specs=[spec], out_specs=spec,
                          out_shape=jax.ShapeDtypeStruct(logf_t.shape, F32),
                          compiler_params=_cp("parallel"), name="negcumsum")(logf_t)


def _memattn_kernel(q_ref, k_ref, v_ref, o_ref):
    for h in range(N_HEADS_M):
        sl = slice(HEAD_DIM_M * h, HEAD_DIM_M * (h + 1))
        kh = k_ref[0, 0, :, h, :].astype(BF16)
        vh = v_ref[0, 0, :, h, :].astype(BF16)
        s = _dot_nt(q_ref[0, :, sl], kh) * (HEAD_DIM_M ** -0.5)
        p = jnp.exp(s - jnp.max(s, axis=1, keepdims=True))
        o = _dot(p.astype(BF16), vh) / jnp.sum(p, axis=1, keepdims=True)
        o_ref[0, :, sl] = o.astype(o_ref.dtype)


def _memattn(q, k5, v5, layer, tq):
    b, t, w = q.shape
    kv = pl.BlockSpec((1, 1, N_MEM, N_HEADS_M, HEAD_DIM_M), lambda bb, i: (layer, bb, 0, 0, 0))
    qs = pl.BlockSpec((1, tq, w), lambda bb, i: (bb, i, 0))
    return pl.pallas_call(_memattn_kernel, grid=(b, t // tq), in_specs=[qs, kv, kv], out_specs=qs,
                          out_shape=jax.ShapeDtypeStruct(q.shape, BF16),
                          compiler_params=_cp("parallel", "parallel"), name="memattn")(q, k5, v5)


def _score_sample_kernel(nsteps, pps, pt_ref, qi_ref, wb_ref, *refs):
    page_refs, knew_ref, past_o, new_o = refs[:pps], refs[pps], refs[pps + 1], refs[pps + 2]
    j = pl.program_id(1)
    t_new = qi_ref.shape[1] // N_HEADS
    q = qi_ref[0]
    wb = wb_ref[0]

    def score(kp):
        n = kp.shape[1]
        r = jnp.maximum(_dot(q, kp.astype(BF16)), 0.0) * jnp.tile(wb, (1, n // LANES))
        return jnp.sum(r.reshape(N_HEADS, t_new, n), axis=0)

    past_o[0] = score(jnp.concatenate([pr[0, 0] for pr in page_refs], axis=1))

    @pl.when(j == nsteps - 1)
    def _():
        trow = lax.broadcasted_iota(I32, (t_new, LANES), 0)
        lane = lax.broadcasted_iota(I32, (t_new, LANES), 1)
        new_o[0] = jnp.where(lane <= trow, score(knew_ref[0]), -jnp.inf)


def _thr_sample_kernel(past, t_new, ksel, unroll, past_ref, new_ref, o_ref, sc_ref):
    rows = SEL_TQ
    total = past + LANES
    sc_ref[:, :past] = past_ref[...]
    sc_ref[:, past:] = new_ref[...]
    ngrp = total // (LANES * unroll)
    n_adm = past + (lax.broadcasted_iota(I32, (rows, 1), 0) & (t_new - 1)) + 1
    tb, jb = _topk_threshold(sc_ref, ngrp, rows, unroll, n_adm, ksel, (total - 1).bit_length())
    lane = lax.broadcasted_iota(I32, (rows, LANES), 1)

    def write(g, carry):
        for u in range(unroll):
            off = pl.multiple_of((g * unroll + u) * LANES, LANES)
            sel = _selected(sc_ref[:, pl.ds(off, LANES)], off, tb, jb, lane)
            o_ref[:, pl.ds(off, LANES)] = jnp.where(sel, 0.0, NEG)
        return carry

    lax.fori_loop(0, ngrp, write, 0)


def _sel_sample(pt_flat, qi_hs, wb_hs, cache_kidx, layer, knew, npg):
    db = qi_hs.shape[0]
    t_new = qi_hs.shape[1] // N_HEADS
    past = npg * PAGE
    total = past + LANES
    pps = SEL_PPS if npg % SEL_PPS == 0 else 1
    nsteps = npg // pps
    ncol = total // LANES
    unroll = max(u for u in range(1, 9) if ncol % u == 0)
    ksel = min(TOPK_KEYS, (past + t_new) // 4)
    page = lambda r: pl.BlockSpec((1, 1, IDX_DIM, PAGE), lambda b, j, pt: (layer, pt[b * npg + j * pps + r], 0, 0))
    sc_past, sc_new = pl.pallas_call(
        functools.partial(_score_sample_kernel, nsteps, pps),
        grid_spec=pltpu.PrefetchScalarGridSpec(
            num_scalar_prefetch=1, grid=(db, nsteps),
            in_specs=[pl.BlockSpec((1, N_HEADS * t_new, IDX_DIM), lambda b, j, pt: (b, 0, 0)),
                      pl.BlockSpec((1, N_HEADS * t_new, LANES), lambda b, j, pt: (b, 0, 0))]
                     + [page(r) for r in range(pps)]
                     + [pl.BlockSpec((1, IDX_DIM, PAGE), lambda b, j, pt: (b, 0, 0))],
            out_specs=[pl.BlockSpec((1, t_new, pps * PAGE), lambda b, j, pt: (b, 0, j)),
                       pl.BlockSpec((1, t_new, LANES), lambda b, j, pt: (b, 0, 0))]),
        out_shape=[jax.ShapeDtypeStruct((db, t_new, past), F32), jax.ShapeDtypeStruct((db, t_new, LANES), F32)],
        compiler_params=_cp("parallel", "arbitrary"), name="score_sample",
    )(pt_flat, qi_hs, wb_hs, *([cache_kidx] * pps), knew)
    rows = SEL_TQ
    assert (t_new & (t_new - 1)) == 0 and (db * t_new) % rows == 0
    bias = pl.pallas_call(
        functools.partial(_thr_sample_kernel, past, t_new, ksel, unroll), grid=(db * t_new // rows,),
        in_specs=[pl.BlockSpec((rows, past), lambda g: (g, 0)), pl.BlockSpec((rows, LANES), lambda g: (g, 0))],
        out_specs=pl.BlockSpec((rows, total), lambda g: (g, 0)),
        out_shape=jax.ShapeDtypeStruct((db * t_new, total), F32),
        scratch_shapes=[pltpu.VMEM((rows, total), F32)],
        compiler_params=_cp("parallel"), name="thr_sample",
    )(sc_past.reshape(db * t_new, past), sc_new.reshape(db * t_new, LANES))
    return bias.reshape(db, t_new, total)


def _paged_kernel(mode, nsteps, pps, pt_ref, q_ref, *refs):
    k_refs, v_refs, b_refs = refs[:pps], refs[pps:2 * pps], refs[2 * pps:3 * pps]
    kn_ref, vn_ref, bn_ref, o_ref, acc_ref, m_ref, l_ref, coff_ref = refs[3 * pps:]
    j = pl.program_id(1)
    t_new = o_ref.shape[2]
    rows = N_HEADS * t_new
    lane = lax.broadcasted_iota(I32, (N_HEADS, LANES), 1)

    @pl.when(j == 0)
    def _():
        acc_ref[...] = jnp.zeros(acc_ref.shape, F32)
        m_ref[...] = jnp.full(m_ref.shape, NEG, F32)
        l_ref[...] = jnp.zeros(l_ref.shape, F32)
        coff_ref[...] = jnp.zeros(coff_ref.shape, F32)

    def flat(ref_page):
        return ref_page.reshape(W_ATT, PAGE).astype(BF16)

    def attend(kt, vt, bias):
        n = kt.shape[1]
        s = _dot(q_ref[0], kt) + bias
        m_prev = m_ref[...]
        m_new = jnp.maximum(m_prev, jnp.max(s, axis=1, keepdims=True))
        alpha = jnp.exp2(m_prev - m_new)
        p = jnp.exp2(s - jnp.tile(m_new, (1, n // LANES)))
        l_ref[...] = alpha * l_ref[...] + jnp.sum(p, axis=1, keepdims=True)
        m_ref[...] = m_new
        acc_ref[...] = jnp.tile(alpha, (1, W_ATT // LANES)) * acc_ref[...] + _dot_nt(p.astype(BF16), vt)

    def neg_cum(lf):
        c = _lane_cumsum(lf, lane) + coff_ref[...]
        coff_ref[...] = jnp.broadcast_to(c[:, LANES - 1:LANES], (N_HEADS, LANES))
        return c * -LOG2E

    def per_head_rows(x):
        return jnp.broadcast_to(x[:, None, :], (N_HEADS, t_new, x.shape[1])).reshape(rows, x.shape[1])

    def per_token_rows(x):
        return jnp.tile(x, (N_HEADS, 1))

    if mode == "fox":
        bias = jnp.concatenate([per_head_rows(neg_cum(br[0, 0])) for br in b_refs], axis=1)
    else:
        bias = jnp.concatenate([per_token_rows(br[0]) for br in b_refs], axis=1)
    attend(jnp.concatenate([flat(kr[0, 0]) for kr in k_refs], axis=1),
           jnp.concatenate([flat(vr[0, 0]) for vr in v_refs], axis=1), bias)

    @pl.when(j == nsteps - 1)
    def _():
        if mode == "fox":
            tok = lax.broadcasted_iota(I32, (N_HEADS, t_new, LANES), 1).reshape(rows, LANES)
            causal = lax.broadcasted_iota(I32, (rows, LANES), 1) <= tok
            bias_new = jnp.where(causal, per_head_rows(neg_cum(bn_ref[0])), NEG)
        else:
            bias_new = per_token_rows(bn_ref[0])
        attend(flat(kn_ref[0]), flat(vn_ref[0]), bias_new)
        for h in range(N_HEADS):
            rs = slice(t_new * h, t_new * (h + 1))
            o_ref[0, h] = (acc_ref[rs, HEAD_DIM * h:HEAD_DIM * (h + 1)] / l_ref[rs, :HEAD_DIM]).astype(o_ref.dtype)


def _block_diag_q(q_hs):
    db, nh, t_new, d = q_hs.shape
    eye = jnp.eye(nh, dtype=q_hs.dtype)
    return (q_hs[:, :, :, None, :] * eye[None, :, None, :, None]).reshape(db, nh * t_new, nh * d)


def _paged(mode, pt_flat, q_hs, cache_k, cache_v, layer, bias_src, knew, vnew, bias_new, npg):
    db, _, t_new, _ = q_hs.shape
    assert t_new == N_HEADS
    pps = ATT_PPS if npg % ATT_PPS == 0 else 1
    nsteps = npg // pps
    rows = N_HEADS * t_new
    pidx = lambda r: (lambda b, j, pt: (layer, pt[b * npg + j * pps + r], 0, 0, 0))
    page = lambda r: pl.BlockSpec((1, 1, N_HEADS, HEAD_DIM, PAGE), pidx(r))
    if mode == "fox":
        bspec = lambda r: pl.BlockSpec((1, 1, N_HEADS, PAGE), lambda b, j, pt: (layer, pt[b * npg + j * pps + r], 0, 0))
        bnew = pl.BlockSpec((1, N_HEADS, LANES), lambda b, j, pt: (b, 0, 0))
    else:
        bspec = lambda r: pl.BlockSpec((1, t_new, PAGE), lambda b, j, pt: (b, 0, j * pps + r))
        bnew = pl.BlockSpec((1, t_new, LANES), lambda b, j, pt: (b, 0, npg))
    newp = pl.BlockSpec((1, N_HEADS, HEAD_DIM, PAGE), lambda b, j, pt: (b, 0, 0, 0))
    qs = pl.BlockSpec((1, rows, W_ATT), lambda b, j, pt: (b, 0, 0))
    return pl.pallas_call(
        functools.partial(_paged_kernel, mode, nsteps, pps),
        grid_spec=pltpu.PrefetchScalarGridSpec(
            num_scalar_prefetch=1, grid=(db, nsteps),
            in_specs=[qs] + [page(r) for r in range(pps)] + [page(r) for r in range(pps)]
                     + [bspec(r) for r in range(pps)] + [newp, newp, bnew],
            out_specs=pl.BlockSpec((1, N_HEADS, t_new, HEAD_DIM), lambda b, j, pt: (b, 0, 0, 0)),
            scratch_shapes=[pltpu.VMEM((rows, W_ATT), F32), pltpu.VMEM((rows, LANES), F32),
                            pltpu.VMEM((rows, LANES), F32), pltpu.VMEM((N_HEADS, LANES), F32)]),
        out_shape=jax.ShapeDtypeStruct(q_hs.shape, BF16),
        compiler_params=_cp("parallel", "arbitrary"), name="paged_" + mode,
    )(pt_flat, _block_diag_q(q_hs), *([cache_k] * pps), *([cache_v] * pps), *([bias_src] * pps), knew, vnew, bias_new)


def _blockout_kernel(x_ref, oa_ref, ob_ref, om_ref, gmix_ref, wg_ref, wpa_ref, wpb_ref, wpm_ref, wo_ref, gffn_ref,
                     wr_ref, br_ref, x1_o, h2_o, te_o, tg_o):
    x = x_ref[...]
    tm = x.shape[0]
    h = _rms_rows(x, gmix_ref[...]).astype(BF16)
    merged = jnp.zeros((tm, D_MODEL), F32)
    for n, (o_ref, wp_ref) in enumerate(((oa_ref, wpa_ref), (ob_ref, wpb_ref), (om_ref, wpm_ref))):
        gate = jax.nn.sigmoid(_dot(h, wg_ref[:, D_MODEL * n:D_MODEL * (n + 1)]))
        merged = merged + gate * _dot(o_ref[...], wp_ref[...])
    x1 = x + _dot(merged.astype(BF16), wo_ref[...])
    x1_o[...] = x1
    h2 = _rms_rows(x1, gffn_ref[...])
    for c in range(D_MODEL // LANES):
        h2_o[:, c, :] = h2[:, LANES * c:LANES * (c + 1)]
    logits = _dot(h2.astype(BF16), wr_ref[...]) + br_ref[...]
    lane = lax.broadcasted_iota(I32, (tm, LANES), 1)
    vals, te = [], jnp.zeros((tm, LANES), I32)
    for k in range(TOP_K):
        m = jnp.max(logits, axis=1, keepdims=True)
        ix = jnp.min(jnp.where(logits == m, lane, LANES), axis=1, keepdims=True)
        vals.append(m)
        te = jnp.where(lane == k, ix, te)
        logits = jnp.where(lane == ix, -jnp.inf, logits)
    ex = [jnp.exp(v - vals[0]) for v in vals]
    den = ex[0] + ex[1] + ex[2] + ex[3]
    tg = jnp.zeros((tm, LANES), F32)
    for k in range(TOP_K):
        tg = jnp.where(lane == k, ex[k] / den, tg)
    te_o[...] = te
    tg_o[...] = tg


def _blockout(x2d, oa, ob, om, wts):
    n = x2d.shape[0]
    tm = OUT_TM
    row = lambda w: pl.BlockSpec((tm, w), lambda i: (i, 0))
    names = ("g_mix", "wg", "wpa", "wpb", "wpm", "wo", "g_ffn", "wr", "br")
    nch = D_MODEL // LANES
    return pl.pallas_call(
        _blockout_kernel, grid=(n // tm,),
        in_specs=[row(D_MODEL), row(W_ATT), row(W_ATT), row(N_HEADS_M * HEAD_DIM_M)] + [_full_spec(wts[k]) for k in names],
        out_specs=[row(D_MODEL), pl.BlockSpec((tm, nch, LANES), lambda i: (i, 0, 0)), row(LANES), row(LANES)],
        out_shape=[jax.ShapeDtypeStruct((n, D_MODEL), F32), jax.ShapeDtypeStruct((n, nch, LANES), F32),
                   jax.ShapeDtypeStruct((n, LANES), I32), jax.ShapeDtypeStruct((n, LANES), F32)],
        compiler_params=_cp("parallel"), name="blockout")(x2d, oa, ob, om, *[wts[k] for k in names])


def _row_copy(src_ref, dst_ref, src_row, dst_row, sem):
    return pltpu.make_async_copy(src_ref.at[src_row], dst_ref.at[dst_row], sem)


def _gather_kernel(idx_ref, src_ref, out_ref, sem):
    def issue(r, c):
        _row_copy(src_ref, out_ref, idx_ref[r], r, sem).start()
        return c

    lax.fori_loop(0, GATHER_R, issue, 0)

    def wait(r, c):
        _row_copy(src_ref, out_ref, 0, r, sem).wait()
        return c

    lax.fori_loop(0, GATHER_R, wait, 0)


def _gather_rows(idx, src):
    n = idx.shape[0]
    return pl.pallas_call(
        _gather_kernel, grid=(n // GATHER_R,),
        in_specs=[pl.BlockSpec((GATHER_R,), lambda i: (i,), memory_space=pltpu.SMEM),
                  pl.BlockSpec(memory_space=pl.ANY)],
        out_specs=pl.BlockSpec((GATHER_R,) + src.shape[1:], lambda i: (i, 0, 0)),
        out_shape=jax.ShapeDtypeStruct((n,) + src.shape[1:], src.dtype),
        scratch_shapes=[pltpu.SemaphoreType.DMA(())],
        compiler_params=_cp("arbitrary"), name="moe_gather")(idx, src)


def _ffn_kernel(be_ref, na_ref, x_ref, wu_ref, bu_ref, wd_ref, bd_ref, y_ref):
    nch = D_MODEL // LANES
    i = pl.program_id(0)

    @pl.when(i < na_ref[0])
    def _():
        x = jnp.concatenate([x_ref[:, c, :] for c in range(nch)], axis=1).astype(BF16)
        u = _dot(x, wu_ref[0]) + bu_ref[0]
        glu = jnp.minimum(u[:, :D_FF], SWIGLU_LIMIT)
        lin = jnp.clip(u[:, D_FF:], -SWIGLU_LIMIT, SWIGLU_LIMIT)
        a = glu * jax.nn.sigmoid(SWIGLU_ALPHA * glu) * (lin + 1.0)
        y = _dot(a.astype(BF16), wd_ref[0]) + bd_ref[0]
        for c in range(nch):
            y_ref[:, c, :] = y[:, LANES * c:LANES * (c + 1)]

    @pl.when(i >= na_ref[0])
    def _():
        y_ref[...] = jnp.zeros(y_ref.shape, F32)


def _ffn(blk_expert, n_active, xs, w_up, b_up, w_down, b_down):
    n = xs.shape[0]
    nch = D_MODEL // LANES
    xspec = pl.BlockSpec((MOE_BLK, nch, LANES), lambda i, be, na: (i, 0, 0))
    ex = lambda shape: pl.BlockSpec((1,) + shape, lambda i, be, na: (be[i], 0, 0))
    return pl.pallas_call(
        _ffn_kernel,
        grid_spec=pltpu.PrefetchScalarGridSpec(
            num_scalar_prefetch=2, grid=(n // MOE_BLK,),
            in_specs=[xspec, ex((D_MODEL, 2 * D_FF)), ex((1, 2 * D_FF)), ex((D_FF, D_MODEL)), ex((1, D_MODEL))],
            out_specs=xspec),
        out_shape=jax.ShapeDtypeStruct(xs.shape, F32),
        compiler_params=_cp("arbitrary"), name="moe_ffn")(blk_expert, n_active, xs, w_up, b_up, w_down, b_down)


def _combine_kernel(idx_ref, ys_ref, x1_ref, g_ref, o_ref, buf_ref, sem):
    tm = COMB_TM
    nch = D_MODEL // LANES

    def issue(r, c):
        _row_copy(ys_ref, buf_ref, idx_ref[r], r, sem).start()
        return c

    lax.fori_loop(0, TOP_K * tm, issue, 0)

    def wait(r, c):
        _row_copy(ys_ref, buf_ref, 0, r, sem).wait()
        return c

    lax.fori_loop(0, TOP_K * tm, wait, 0)
    g = g_ref[...]
    y = x1_ref[...]
    for k in range(TOP_K):
        rows = jnp.concatenate([buf_ref[k * tm:(k + 1) * tm, c, :] for c in range(nch)], axis=1)
        y = y + g[:, k:k + 1] * rows
    o_ref[...] = y


def _combine(dest_tiles, ys, x1, gates):
    n = x1.shape[0]
    tm = COMB_TM
    row = lambda w: pl.BlockSpec((tm, w), lambda i: (i, 0))
    return pl.pallas_call(
        _combine_kernel, grid=(n // tm,),
        in_specs=[pl.BlockSpec((TOP_K * tm,), lambda i: (i,), memory_space=pltpu.SMEM),
                  pl.BlockSpec(memory_space=pl.ANY), row(D_MODEL), row(LANES)],
        out_specs=row(D_MODEL),
        out_shape=jax.ShapeDtypeStruct((n, D_MODEL), F32),
        scratch_shapes=[pltpu.VMEM((TOP_K * tm,) + ys.shape[1:], F32), pltpu.SemaphoreType.DMA(())],
        compiler_params=_cp("arbitrary"), name="moe_combine")(dest_tiles, ys, x1, gates)


def _moe(x1, h2, top_e, top_g, wts):
    n = x1.shape[0]
    n_assign = n * TOP_K
    flat_e = top_e[:, :TOP_K].reshape(-1)
    onehot = (flat_e[:, None] == jnp.arange(N_EXPERTS, dtype=I32)[None, :]).astype(I32)
    csum = jnp.cumsum(onehot, axis=0)
    rank = jnp.take_along_axis(csum, flat_e[:, None], axis=1)[:, 0] - 1
    counts = csum[-1]
    padded = (counts + MOE_BLK - 1) // MOE_BLK * MOE_BLK
    pad_end = jnp.cumsum(padded)
    dest = (pad_end - padded)[flat_e] + rank
    n_blocks = -(-n_assign // MOE_BLK) + N_EXPERTS
    n_rows = -(-(n_blocks * MOE_BLK) // GATHER_R) * GATHER_R
    row_tok = jnp.zeros((n_rows,), I32).at[dest].set(jnp.arange(n_assign, dtype=I32) // TOP_K)
    nb = n_rows // MOE_BLK
    blk_expert = jnp.minimum(jnp.searchsorted(pad_end, jnp.arange(nb, dtype=I32) * MOE_BLK, side="right"),
                             N_EXPERTS - 1).astype(I32)
    n_active = (pad_end[-1:] // MOE_BLK).astype(I32)
    xs = _gather_rows(row_tok, h2)
    ys = _ffn(blk_expert, n_active, xs, wts["w_up"], wts["b_up"], wts["w_down"], wts["b_down"])
    dest_tiles = dest.reshape(n // COMB_TM, COMB_TM, TOP_K).transpose(0, 2, 1).reshape(-1)
    return _combine(dest_tiles, ys, x1, top_g)


def _block_diag(width, blk):
    r = jnp.arange(width) // blk
    return jnp.where(r[:, None] == r[None, :], 1.0 / blk, 0.0).astype(BF16)


def _pad_cols(a, width):
    return jnp.pad(a, ((0, 0), (0, width - a.shape[1])))


def _layer_weights(l, w_in, b_forget, g_mix, g_qa, g_ka, g_kidx, g_qb, g_kb, g_qm, g_mem, w_mem_kv, g_km, w_pa, w_pb,
                   w_pm, w_o, g_ffn, w_router, b_router, w_up, b_up, w_down, b_down):
    pts = [0]
    for s in SPLIT_SIZES:
        pts.append(pts[-1] + s)
    col = lambda n: w_in[l][:, pts[n]:pts[n + 1]]
    qa, ka, va, qi, ki, wi, qb, kb, vb, fb, qm, ga, gb, gm = [col(n) for n in range(len(SPLIT_SIZES))]
    rowv = lambda v: v.astype(F32)[None, :]
    scale = HEAD_DIM ** -0.5 * LOG2E
    return dict(
        wbig=jnp.concatenate([qa, ka, va, qi, qb, kb, vb, qm], axis=1).astype(BF16),
        wsm=jnp.concatenate([_pad_cols(ki, LANES), _pad_cols(wi, LANES), _pad_cols(fb, LANES)], axis=1).astype(BF16),
        wg=jnp.concatenate([ga, gb, gm], axis=1).astype(BF16),
        g_mix=rowv(g_mix[l]), g_ffn=rowv(g_ffn[l]), g_mem=rowv(g_mem[l]),
        gqa=rowv(jnp.tile(g_qa[l] * scale, N_HEADS)), gka=rowv(jnp.tile(g_ka[l], N_HEADS)),
        gqb=rowv(jnp.tile(g_qb[l] * scale, N_HEADS)), gkb=rowv(jnp.tile(g_kb[l], N_HEADS)),
        gqm=rowv(jnp.tile(g_qm[l], N_HEADS_M)), gkm=rowv(jnp.tile(g_km[l], N_HEADS_M)),
        gki=_pad_cols(rowv(g_kidx[l]), LANES), bf=_pad_cols(rowv(b_forget[l]), LANES),
        bd64=_block_diag(W_ATT, HEAD_DIM), bd128=_block_diag(N_HEADS_M * HEAD_DIM_M, HEAD_DIM_M),
        w_mem_kv=w_mem_kv[l].astype(BF16),
        wpa=w_pa[l].astype(BF16), wpb=w_pb[l].astype(BF16), wpm=w_pm[l].astype(BF16), wo=w_o[l].astype(BF16),
        wr=_pad_cols(w_router[l], LANES).astype(BF16),
        br=jnp.pad(rowv(b_router[l]), ((0, 0), (0, LANES - N_EXPERTS)), constant_values=NEG),
        w_up=w_up[l].astype(BF16), b_up=b_up[l].astype(F32)[:, None, :],
        w_down=w_down[l].astype(BF16), b_down=b_down[l].astype(F32)[:, None, :],
    )


def _finish_block(x2d, oa, ob, om, wts):
    x1, h2, te, tg = _blockout(x2d, oa, ob, om, wts)
    return _moe(x1, h2, te, tg, wts)


def kernel(x_prompt, x_sample, mem_prompt, cache_a_k, cache_a_v, cache_a_kidx, cache_b_k, cache_b_v, cache_b_logf,
           cache_mem_k, cache_mem_v, page_table, w_in, b_forget, g_mix, g_qa, g_ka, g_kidx, g_qb, g_kb, g_qm, g_mem,
           w_mem_kv, g_km, w_pa, w_pb, w_pm, w_o, g_ffn, w_router, b_router, w_up, b_up, w_down, b_down):
    depth = w_in.shape[0]
    b, seq, _ = x_prompt.shape
    db, t_new, _ = x_sample.shape
    npg = page_table.shape[1]
    past = npg * PAGE
    pt_flat = page_table.reshape(-1).astype(I32)
    tabs_p = _rope_tables(jnp.arange(seq))
    tabs_s = _rope_tables(past + jnp.arange(PROJ_TM) % t_new)
    page_t = lambda c: jnp.transpose(c, (0, 1, 3, 4, 2))
    cak_t, cav_t, cbk_t, cbv_t = page_t(cache_a_k), page_t(cache_a_v), page_t(cache_b_k), page_t(cache_b_v)
    kidx_t = jnp.swapaxes(cache_a_kidx, 2, 3)
    logf_cache_t = jnp.swapaxes(cache_b_logf, 2, 3)
    hp = x_prompt.reshape(b * seq, D_MODEL)
    hs = x_sample.reshape(db * t_new, D_MODEL)
    p_states, s_states = [], []
    heads = lambda t, nb, nt: t.reshape(nb, nt, N_HEADS, HEAD_DIM)
    for l in range(depth):
        wts = _layer_weights(l, w_in, b_forget, g_mix, g_qa, g_ka, g_kidx, g_qb, g_kb, g_qm, g_mem, w_mem_kv, g_km,
                             w_pa, w_pb, w_pm, w_o, g_ffn, w_router, b_router, w_up, b_up, w_down, b_down)
        pr = _proj(hp, tabs_p, seq // PROJ_TM, wts)
        mk, mv = _memkv(mem_prompt.reshape(b * N_MEM, D_MODEL), wts["g_mem"], wts["w_mem_kv"], wts["gkm"], wts["bd128"])
        mk5 = mk.reshape(1, b, N_MEM, N_HEADS_M, HEAD_DIM_M)
        mv5 = mv.reshape(1, b, N_MEM, N_HEADS_M, HEAD_DIM_M)
        r3 = lambda t: t.reshape(b, seq, t.shape[-1])
        sel_bias = _sel_prompt(r3(pr["qi"]), r3(pr["wi"]), r3(pr["ki2"]))
        o_a = _flash("dsa", r3(pr["qa"]), r3(pr["ka_bf"]), r3(pr["va_bf"]), sel_bias)
        negc = _negcumsum(jnp.swapaxes(r3(pr["logf"]), 1, 2))
        o_b = _flash("fox", r3(pr["qb"]), r3(pr["kb_bf"]), r3(pr["vb_bf"]), negc)
        o_m = _memattn(r3(pr["qm"]), mk5, mv5, 0, MEM_TQ)
        rf = lambda t: t.reshape(b * seq, t.shape[-1])
        p_states.append((heads(pr["ka"], b, seq), heads(pr["va"], b, seq), r3(pr["ki"]), heads(pr["kb"], b, seq),
                         heads(pr["vb"], b, seq), r3(pr["logf"]), mk5[0], mv5[0]))
        hp = _finish_block(hp, rf(o_a), rf(o_b), rf(o_m), wts)
        sr = _proj(hs, tabs_s, 1, wts)
        hs_t = lambda t, d: jnp.swapaxes(t.reshape(db, t_new, N_HEADS, d), 1, 2)
        new_page = lambda t: jnp.pad(jnp.transpose(heads(t, db, t_new), (0, 2, 3, 1)),
                                     ((0, 0), (0, 0), (0, 0), (0, PAGE - t_new)))
        qi_hs = hs_t(sr["qi"], IDX_DIM).reshape(db, N_HEADS * t_new, IDX_DIM)
        wb_hs = jnp.broadcast_to(jnp.swapaxes(sr["wi"].reshape(db, t_new, N_HEADS), 1, 2).reshape(db, N_HEADS * t_new, 1),
                                 (db, N_HEADS * t_new, LANES))
        ki_new = jnp.pad(jnp.swapaxes(sr["ki"].reshape(db, t_new, IDX_DIM), 1, 2), ((0, 0), (0, 0), (0, PAGE - t_new)))
        sel_s = _sel_sample(pt_flat, qi_hs, wb_hs, kidx_t, l, ki_new, npg)
        o_a = _paged("dsa", pt_flat, hs_t(sr["qa"], HEAD_DIM), cak_t, cav_t, l, sel_s,
                     new_page(sr["ka"]), new_page(sr["va"]), sel_s, npg)
        lf_new = jnp.pad(jnp.swapaxes(sr["logf"].reshape(db, t_new, N_HEADS), 1, 2), ((0, 0), (0, 0), (0, LANES - t_new)))
        o_b = _paged("fox", pt_flat, hs_t(sr["qb"], HEAD_DIM), cbk_t, cbv_t, l, logf_cache_t,
                     new_page(sr["kb"]), new_page(sr["vb"]), lf_new, npg)
        o_m = _memattn(sr["qm"].reshape(db, t_new, -1), cache_mem_k, cache_mem_v, l, t_new)
        un_hs = lambda o: jnp.swapaxes(o, 1, 2).reshape(db * t_new, W_ATT)
        s_states.append((heads(sr["ka"], db, t_new), heads(sr["va"], db, t_new), sr["ki"].reshape(db, t_new, IDX_DIM),
                         heads(sr["kb"], db, t_new), heads(sr["vb"], db, t_new), sr["logf"].reshape(db, t_new, N_HEADS)))
        hs = _finish_block(hs, un_hs(o_a), un_hs(o_b), o_m.reshape(db * t_new, -1), wts)
    stack = lambda states: [jnp.stack(s, 0) for s in zip(*states)]
    return (hp.reshape(b, seq, D_MODEL), hs.reshape(db, t_new, D_MODEL), *stack(p_states), *stack(s_states))
```

```python
import functools

import jax
import jax.numpy as jnp
from jax import lax
from jax.experimental import pallas as pl
from jax.experimental.pallas import tpu as pltpu

F32, BF16, I32 = jnp.float32, jnp.bfloat16, jnp.int32

D_MODEL = 1024
HEAD_DIM = 64
N_HEADS = 8
W_ATT = N_HEADS * HEAD_DIM
N_HEADS_M = 4
HEAD_DIM_M = 128
IDX_DIM = 64
TOPK_KEYS = 256
N_MEM = 256
ROPE_THETA = 500000.0
ROT_DIM = HEAD_DIM // 4
ROT_HALF = ROT_DIM // 2
N_EXPERTS = 32
TOP_K = 4
D_FF = 1024
SWIGLU_LIMIT = 7.0
SWIGLU_ALPHA = 1.702
PAGE = 128
EPS = 1e-6
IDX_SCALE = (N_HEADS * IDX_DIM) ** -0.5
SPLIT_SIZES = (W_ATT, W_ATT, W_ATT, N_HEADS * IDX_DIM, IDX_DIM, N_HEADS,
               W_ATT, W_ATT, W_ATT, N_HEADS, N_HEADS_M * HEAD_DIM_M, D_MODEL, D_MODEL, D_MODEL)

LANES = 128
NEG = -1e30
FLT_MAX = 3.4028234663852886e38
LOG2E = 1.4426950408889634
INT_MIN = -2147483648
BIG_IDX = 1 << 30
VMEM_LIMIT = 56 * 1024 * 1024

PROJ_TM = 256
FLASH_TQ, FLASH_TK = 1024, 1024
FLASH_SQ, FLASH_SK = 1024, 1024
SEL_TQ, SEL_CH = 128, 512
THR_ROWS = 128
MEM_TQ = 512
OUT_TM = 256
MOE_BLK = 256
GATHER_R = 512
COMB_TM = 256
SEL_PPS = 8
ATT_PPS = 16


def _cp(*sem):
    return pltpu.CompilerParams(dimension_semantics=sem, vmem_limit_bytes=VMEM_LIMIT)


def _dot(a, b):
    return jnp.dot(a, b, preferred_element_type=F32)


def _dot_nt(a, b):
    return lax.dot_general(a, b, (((1,), (1,)), ((), ())), preferred_element_type=F32)


def _full_spec(a):
    nd = a.ndim
    return pl.BlockSpec(a.shape, lambda *_: (0,) * nd)


def _rms_rows(x, g):
    return x * lax.rsqrt(jnp.mean(x * x, axis=-1, keepdims=True) + EPS) * g


def _proj_kernel(x_ref, gmix_ref, wbig_ref, wsm_ref, cos_ref, sa_ref, sb_ref, gqa_ref, gka_ref, gqb_ref, gkb_ref,
                 gqm_ref, gki_ref, bf_ref, bd64_ref, bd128_ref,
                 qa_o, ka_o, kab_o, va_o, vab_o, qi_o, ki_o, ki2_o, wi_o, qb_o, kb_o, kbb_o, vb_o, vbb_o, lf_o, qm_o):
    h = _rms_rows(x_ref[...], gmix_ref[...]).astype(BF16)
    cos, sa, sb = cos_ref[...], sa_ref[...], sb_ref[...]

    def seg(j):
        return _dot(h, wbig_ref[:, W_ATT * j:W_ATT * (j + 1)])

    def hnorm(y, bd_ref, g_ref):
        ms = _dot((y * y).astype(BF16), bd_ref[...])
        return y * lax.rsqrt(ms + EPS) * g_ref[...]

    def rope(y):
        parts = []
        for j in range(y.shape[1] // LANES):
            yc = y[:, LANES * j:LANES * (j + 1)]
            parts.append(yc * cos + pltpu.roll(yc, LANES - ROT_HALF, 1) * sa + pltpu.roll(yc, ROT_HALF, 1) * sb)
        return parts[0] if len(parts) == 1 else jnp.concatenate(parts, axis=1)

    qa_o[...] = rope(hnorm(seg(0), bd64_ref, gqa_ref)).astype(BF16)
    ka = rope(hnorm(seg(1), bd64_ref, gka_ref))
    ka_o[...] = ka
    kab_o[...] = ka.astype(BF16)
    va = seg(2)
    va_o[...] = va
    vab_o[...] = va.astype(BF16)
    qi_o[...] = rope(seg(3)).astype(BF16)
    qb_o[...] = hnorm(seg(4), bd64_ref, gqb_ref).astype(BF16)
    kb = hnorm(seg(5), bd64_ref, gkb_ref)
    kb_o[...] = kb
    kbb_o[...] = kb.astype(BF16)
    vb = seg(6)
    vb_o[...] = vb
    vbb_o[...] = vb.astype(BF16)
    qm_o[...] = hnorm(seg(7), bd128_ref, gqm_ref).astype(BF16)

    ys = _dot(h, wsm_ref[:, 0:LANES])
    ms = jnp.sum(ys * ys, axis=-1, keepdims=True) * (1.0 / IDX_DIM)
    ki = rope(ys * lax.rsqrt(ms + EPS) * gki_ref[...])
    ki_o[...] = ki[:, :IDX_DIM]
    ki2_o[...] = (ki + pltpu.roll(ki, IDX_DIM, 1)).astype(BF16)
    wi_o[...] = _dot(h, wsm_ref[:, LANES:2 * LANES])[:, :N_HEADS] * IDX_SCALE
    z = _dot(h, wsm_ref[:, 2 * LANES:3 * LANES]) + bf_ref[...]
    lf_o[...] = (jnp.minimum(z, 0.0) - jnp.log1p(jnp.exp(-jnp.abs(z))))[:, :N_HEADS]


def _proj(x2d, tabs, nper, wts):
    n = x2d.shape[0]
    tm = PROJ_TM
    row = lambda w: pl.BlockSpec((tm, w), lambda i: (i, 0))
    tab = pl.BlockSpec((tm, LANES), lambda i: (i % nper, 0))
    outs = [(W_ATT, BF16), (W_ATT, F32), (W_ATT, BF16), (W_ATT, F32), (W_ATT, BF16), (W_ATT, BF16), (IDX_DIM, F32),
            (LANES, BF16), (N_HEADS, F32), (W_ATT, BF16), (W_ATT, F32), (W_ATT, BF16), (W_ATT, F32), (W_ATT, BF16),
            (N_HEADS, F32), (W_ATT, BF16)]
    consts = [wts[k] for k in ("g_mix", "wbig", "wsm")] + list(tabs) + [
        wts[k] for k in ("gqa", "gka", "gqb", "gkb", "gqm", "gki", "bf", "bd64", "bd128")]
    in_specs = [row(D_MODEL)] + [_full_spec(wts["g_mix"]), _full_spec(wts["wbig"]), _full_spec(wts["wsm"]), tab, tab, tab] + [
        _full_spec(wts[k]) for k in ("gqa", "gka", "gqb", "gkb", "gqm", "gki", "bf", "bd64", "bd128")]
    names = ("qa", "ka", "ka_bf", "va", "va_bf", "qi", "ki", "ki2", "wi", "qb", "kb", "kb_bf", "vb", "vb_bf", "logf", "qm")
    res = pl.pallas_call(
        _proj_kernel, grid=(n // tm,), in_specs=in_specs,
        out_specs=[row(w) for w, _ in outs],
        out_shape=[jax.ShapeDtypeStruct((n, w), dt) for w, dt in outs],
        compiler_params=_cp("parallel"), name="proj")(x2d, *consts)
    return dict(zip(names, res))


def _rope_tables(pos):
    d = jnp.arange(LANES) % HEAD_DIM
    inv_freq = ROPE_THETA ** (-(d % ROT_HALF).astype(F32) * 2.0 / ROT_DIM)
    ang = pos.astype(F32)[:, None] * inv_freq[None, :]
    cos = jnp.where(d < ROT_DIM, jnp.cos(ang), 1.0)
    sa = jnp.where(d < ROT_HALF, -jnp.sin(ang), 0.0)
    sb = jnp.where((d >= ROT_HALF) & (d < ROT_DIM), jnp.sin(ang), 0.0)
    return cos.astype(F32), sa.astype(F32), sb.astype(F32)


def _memkv_kernel(x_ref, g_ref, w_ref, gkm_ref, bd_ref, k_o, v_o):
    h = _rms_rows(x_ref[...], g_ref[...]).astype(BF16)
    wk = N_HEADS_M * HEAD_DIM_M
    y = _dot(h, w_ref[:, :wk])
    ms = _dot((y * y).astype(BF16), bd_ref[...])
    k_o[...] = y * lax.rsqrt(ms + EPS) * gkm_ref[...]
    v_o[...] = _dot(h, w_ref[:, wk:])


def _memkv(mem2d, g_mem, w_kv, gkm4, bd128):
    n = mem2d.shape[0]
    tm = N_MEM
    wk = N_HEADS_M * HEAD_DIM_M
    row = lambda w: pl.BlockSpec((tm, w), lambda i: (i, 0))
    return pl.pallas_call(
        _memkv_kernel, grid=(n // tm,),
        in_specs=[row(D_MODEL), _full_spec(g_mem), _full_spec(w_kv), _full_spec(gkm4), _full_spec(bd128)],
        out_specs=[row(wk), row(wk)],
        out_shape=[jax.ShapeDtypeStruct((n, wk), F32)] * 2,
        compiler_params=_cp("parallel"), name="memkv")(mem2d, g_mem, w_kv, gkm4, bd128)


def _key_to_f32(k):
    bits = k ^ ((k >> 31) & jnp.int32(0x7FFFFFFF))
    return lax.bitcast_convert_type(bits, F32)


def _count(sc_ref, ngrp, rows, unroll, pred):
    def body(g, acc):
        for u in range(unroll):
            off = pl.multiple_of((g * unroll + u) * LANES, LANES)
            acc = acc + pred(sc_ref[:, pl.ds(off, LANES)], off).astype(I32)
        return acc
    acc = lax.fori_loop(0, ngrp, body, jnp.zeros((rows, LANES), I32))
    return jnp.sum(acc, axis=1, keepdims=True)


def _topk_threshold(sc_ref, ngrp, rows, unroll, n_adm, ksel, idx_bits):
    bc = lambda v: jnp.broadcast_to(v, (rows, LANES))

    def count_ge(thr):
        tb = bc(thr)
        return _count(sc_ref, ngrp, rows, unroll, lambda v, off: v >= tb)

    key = jnp.where(count_ge(jnp.zeros((rows, 1), F32)) >= ksel, 0, INT_MIN).astype(I32)

    def bit_step(it, key):
        cand = key | jnp.left_shift(jnp.int32(1), 30 - it)
        return jnp.where(count_ge(_key_to_f32(cand)) >= ksel, cand, key)

    key = lax.fori_loop(0, 31, bit_step, key)
    all_adm = n_adm <= ksel
    thr = jnp.where(all_adm, -FLT_MAX, _key_to_f32(key))
    tb = bc(thr)
    n_gt = _count(sc_ref, ngrp, rows, unroll, lambda v, off: v > tb)
    n_ge = _count(sc_ref, ngrp, rows, unroll, lambda v, off: v >= tb)
    need = ksel - n_gt
    tie = (n_ge > ksel) & jnp.logical_not(all_adm)
    lane = lax.broadcasted_iota(I32, (rows, LANES), 1)

    def tie_search(_):
        def step(it, x):
            cand = x | jnp.left_shift(jnp.int32(1), idx_bits - 1 - it)
            cb = bc(cand)
            cnt = _count(sc_ref, ngrp, rows, unroll, lambda v, off: (v == tb) & ((off + lane) < cb))
            return jnp.where(cnt < need, cand, x)
        return lax.fori_loop(0, idx_bits, step, jnp.zeros((rows, 1), I32))

    any_tie = jnp.max(tie.astype(I32)) > 0
    jcut = lax.cond(any_tie, tie_search, lambda _: jnp.full((rows, 1), BIG_IDX, I32), 0)
    return tb, bc(jnp.where(tie, jcut, BIG_IDX))


def _selected(v, off, tb, jb, lane):
    return (v > tb) | ((v == tb) & ((off + lane) <= jb))


def _sel_prompt_kernel(seq, ksel, qi_ref, wi_ref, ki_ref, o_ref, sc_ref, wb_ref):
    tq, ch = SEL_TQ, SEL_CH
    unroll = ch // LANES
    i = pl.program_id(1)
    ngrp = ((i + 1) * tq + ch - 1) // ch
    w = wi_ref[0]
    for h in range(N_HEADS):
        wb_ref[h] = jnp.broadcast_to(w[:, h:h + 1], (tq, LANES))
    lo = lax.broadcasted_iota(I32, (tq, LANES), 1) < HEAD_DIM
    zero = jnp.zeros((tq, LANES), BF16)
    qh = []
    for pr in range(N_HEADS // 2):
        q2 = qi_ref[0, :, LANES * pr:LANES * (pr + 1)]
        qh += [jnp.where(lo, q2, zero), jnp.where(lo, zero, q2)]
    rowg = i * tq + lax.broadcasted_iota(I32, (tq, ch), 0)
    coll = lax.broadcasted_iota(I32, (tq, ch), 1)

    def chunk(c, carry):
        off = pl.multiple_of(c * ch, ch)
        kc = ki_ref[0, pl.ds(off, ch), :]
        acc = jnp.zeros((tq, ch), F32)
        for h in range(N_HEADS):
            acc = acc + jnp.maximum(_dot_nt(qh[h], kc), 0.0) * jnp.tile(wb_ref[h], (1, unroll))
        sc_ref[:, pl.ds(off, ch)] = jnp.where(off + coll <= rowg, acc, -jnp.inf)
        return carry

    lax.fori_loop(0, ngrp, chunk, 0)
    n_adm = i * tq + lax.broadcasted_iota(I32, (tq, 1), 0) + 1
    tb, jb = _topk_threshold(sc_ref, ngrp, tq, unroll, n_adm, ksel, (seq - 1).bit_length())
    lane = lax.broadcasted_iota(I32, (tq, LANES), 1)

    def write(g, carry):
        for u in range(unroll):
            off = pl.multiple_of((g * unroll + u) * LANES, LANES)
            sel = _selected(sc_ref[:, pl.ds(off, LANES)], off, tb, jb, lane)
            o_ref[0, :, pl.ds(off, LANES)] = jnp.where(sel, 0.0, NEG).astype(BF16)
        return carry

    lax.fori_loop(0, ngrp, write, 0)

    def fill(g, carry):
        off = pl.multiple_of(g * ch, ch)
        o_ref[0, :, pl.ds(off, ch)] = jnp.full((tq, ch), NEG, BF16)
        return carry

    lax.fori_loop(ngrp, seq // ch, fill, 0)


def _sel_prompt(qi, wi, ki2):
    b, seq, _ = qi.shape
    tq = SEL_TQ
    ksel = min(TOPK_KEYS, seq // 4)
    return pl.pallas_call(
        functools.partial(_sel_prompt_kernel, seq, ksel), grid=(b, seq // tq),
        in_specs=[pl.BlockSpec((1, tq, W_ATT), lambda bb, i: (bb, i, 0)),
                  pl.BlockSpec((1, tq, N_HEADS), lambda bb, i: (bb, i, 0)),
                  pl.BlockSpec((1, seq, LANES), lambda bb, i: (bb, 0, 0))],
        out_specs=pl.BlockSpec((1, tq, seq), lambda bb, i: (bb, i, 0)),
        out_shape=jax.ShapeDtypeStruct((b, seq, seq), BF16),
        scratch_shapes=[pltpu.VMEM((tq, seq), F32), pltpu.VMEM((N_HEADS, tq, LANES), F32)],
        compiler_params=_cp("parallel", "arbitrary"), name="sel_prompt")(qi, wi, ki2)


def _flash_kernel(mode, q_ref, k_ref, v_ref, b_ref, o_ref, acc_ref, m_ref, l_ref):
    tq, tk = FLASH_TQ, FLASH_TK
    sq, sk = FLASH_SQ, FLASH_SK
    i, j = pl.program_id(1), pl.program_id(2)
    last = ((i + 1) * tq - 1) // tk
    lo = lax.broadcasted_iota(I32, (tq, LANES), 1) < HEAD_DIM

    @pl.when(j == 0)
    def _():
        acc_ref[...] = jnp.zeros(acc_ref.shape, F32)
        m_ref[...] = jnp.full(m_ref.shape, NEG, F32)
        l_ref[...] = jnp.zeros(l_ref.shape, F32)

    def sub_tile(diag, r0, c0):
        zero = jnp.zeros((sq, LANES), BF16)
        los = lax.broadcasted_iota(I32, (sq, LANES), 1) < HEAD_DIM
        rows, cols = pl.ds(r0, sq), pl.ds(c0, sk)
        if mode == "dsa":
            bias = b_ref[0, rows, cols].astype(F32)
        if diag:
            causal = ((j * tk + c0 + lax.broadcasted_iota(I32, (sq, sk), 1))
                      <= (i * tq + r0 + lax.broadcasted_iota(I32, (sq, sk), 0)))
        for pr in range(N_HEADS // 2):
            sl = slice(LANES * pr, LANES * (pr + 1))
            q2, k2, v2 = q_ref[0, rows, sl], k_ref[0, cols, sl], v_ref[0, cols, sl]
            accp = acc_ref[rows, sl]
            new = []
            for hh in range(2):
                h = 2 * pr + hh
                qh = jnp.where(los, q2, zero) if hh == 0 else jnp.where(los, zero, q2)
                s = _dot_nt(qh, k2)
                if mode == "dsa":
                    s = s + bias
                else:
                    s = s + b_ref[0, h:h + 1, cols]
                    if diag:
                        s = jnp.where(causal, s, NEG)
                m_prev = m_ref[h, rows, :]
                m_new = jnp.maximum(m_prev, jnp.max(s, axis=1, keepdims=True))
                alpha = jnp.exp2(m_prev - m_new)
                p = jnp.exp2(s - jnp.tile(m_new, (1, sk // LANES)))
                l_ref[h, rows, :] = alpha * l_ref[h, rows, :] + jnp.sum(p, axis=1, keepdims=True)
                m_ref[h, rows, :] = m_new
                new.append(alpha * accp + _dot(p.astype(BF16), v2))
            acc_ref[rows, sl] = jnp.where(los, new[0], new[1])

    def step(diag):
        ncb = tk // sk

        def body(t, carry):
            r0 = pl.multiple_of((t // ncb) * sq, sq)
            c0 = pl.multiple_of((t % ncb) * sk, sk)
            if diag:
                @pl.when(j * tk + c0 <= i * tq + r0 + (sq - 1))
                def _():
                    sub_tile(True, r0, c0)
            else:
                sub_tile(False, r0, c0)
            return carry

        lax.fori_loop(0, (tq // sq) * ncb, body, 0)

    def finish():
        for pr in range(N_HEADS // 2):
            sl = slice(LANES * pr, LANES * (pr + 1))
            o_ref[0, :, sl] = (acc_ref[:, sl] / jnp.where(lo, l_ref[2 * pr], l_ref[2 * pr + 1])).astype(o_ref.dtype)

    if mode == "dsa":
        @pl.when(j <= last)
        def _():
            step(False)
    else:
        @pl.when(j < last)
        def _():
            step(False)

        @pl.when(j == last)
        def _():
            step(True)

    @pl.when(j == last)
    def _():
        finish()


def _flash(mode, q, k, v, bias):
    b, seq, _ = q.shape
    tq, tk = FLASH_TQ, FLASH_TK
    assert tk % tq == 0 and seq % tk == 0
    jc = lambda i, j: jnp.minimum(j, ((i + 1) * tq - 1) // tk)
    if mode == "dsa":
        bspec = pl.BlockSpec((1, tq, tk), lambda bb, i, j: (bb, i, jc(i, j)))
    else:
        bspec = pl.BlockSpec((1, N_HEADS, tk), lambda bb, i, j: (bb, 0, jc(i, j)))
    return pl.pallas_call(
        functools.partial(_flash_kernel, mode), grid=(b, seq // tq, seq // tk),
        in_specs=[pl.BlockSpec((1, tq, W_ATT), lambda bb, i, j: (bb, i, 0)),
                  pl.BlockSpec((1, tk, W_ATT), lambda bb, i, j: (bb, jc(i, j), 0)),
                  pl.BlockSpec((1, tk, W_ATT), lambda bb, i, j: (bb, jc(i, j), 0)),
                  bspec],
        out_specs=pl.BlockSpec((1, tq, W_ATT), lambda bb, i, j: (bb, i, 0)),
        out_shape=jax.ShapeDtypeStruct((b, seq, W_ATT), BF16),
        scratch_shapes=[pltpu.VMEM((tq, W_ATT), F32), pltpu.VMEM((N_HEADS, tq, LANES), F32),
                        pltpu.VMEM((N_HEADS, tq, LANES), F32)],
        compiler_params=_cp("parallel", "parallel", "arbitrary"), name="flash_" + mode)(q, k, v, bias)


def _lane_cumsum(x, lane):
    sh = 1
    while sh < LANES:
        x = x + jnp.where(lane >= sh, pltpu.roll(x, sh, 1), 0.0)
        sh *= 2
    return x


def _negcumsum_kernel(seq, x_ref, o_ref):
    lane = lax.broadcasted_iota(I32, (N_HEADS, LANES), 1)

    def body(c, carry):
        off = pl.multiple_of(c * LANES, LANES)
        cs = _lane_cumsum(x_ref[0, :, pl.ds(off, LANES)], lane) + carry
        o_ref[0, :, pl.ds(off, LANES)] = cs * -LOG2E
        return jnp.broadcast_to(cs[:, LANES - 1:LANES], (N_HEADS, LANES))

    lax.fori_loop(0, seq // LANES, body, jnp.zeros((N_HEADS, LANES), F32))


def _negcumsum(logf_t):
    b, _, seq = logf_t.shape
    spec = pl.BlockSpec((1, N_HEADS, seq), lambda bb: (bb, 0, 0))
    return pl.pallas_call(functools.partial(_negcumsum_kernel, seq), grid=(b,), in_specs=[spec], out_specs=spec,
                          out_shape=jax.ShapeDtypeStruct(logf_t.shape, F32),
                          compiler_params=_cp("parallel"), name="negcumsum")(logf_t)


def _memattn_kernel(q_ref, k_ref, v_ref, o_ref):
    for h in range(N_HEADS_M):
        sl = slice(HEAD_DIM_M * h, HEAD_DIM_M * (h + 1))
        kh = k_ref[0, 0, :, h, :].astype(BF16)
        vh = v_ref[0, 0, :, h, :].astype(BF16)
        s = _dot_nt(q_ref[0, :, sl], kh) * (HEAD_DIM_M ** -0.5)
        p = jnp.exp(s - jnp.max(s, axis=1, keepdims=True))
        o = _dot(p.astype(BF16), vh) / jnp.sum(p, axis=1, keepdims=True)
        o_ref[0, :, sl] = o.astype(o_ref.dtype)


def _memattn(q, k5, v5, layer, tq):
    b, t, w = q.shape
    kv = pl.BlockSpec((1, 1, N_MEM, N_HEADS_M, HEAD_DIM_M), lambda bb, i: (layer, bb, 0, 0, 0))
    qs = pl.BlockSpec((1, tq, w), lambda bb, i: (bb, i, 0))
    return pl.pallas_call(_memattn_kernel, grid=(b, t // tq), in_specs=[qs, kv, kv], out_specs=qs,
                          out_shape=jax.ShapeDtypeStruct(q.shape, BF16),
                          compiler_params=_cp("parallel", "parallel"), name="memattn")(q, k5, v5)


def _score_sample_kernel(nsteps, pps, pt_ref, qi_ref, wb_ref, *refs):
    page_refs, knew_ref, past_o, new_o = refs[:pps], refs[pps], refs[pps + 1], refs[pps + 2]
    j = pl.program_id(1)
    t_new = qi_ref.shape[1] // N_HEADS
    q = qi_ref[0]
    wb = wb_ref[0]

    def score(kp):
        n = kp.shape[1]
        r = jnp.maximum(_dot(q, kp.astype(BF16)), 0.0) * jnp.tile(wb, (1, n // LANES))
        return jnp.sum(r.reshape(N_HEADS, t_new, n), axis=0)

    past_o[0] = score(jnp.concatenate([pr[0, 0] for pr in page_refs], axis=1))

    @pl.when(j == nsteps - 1)
    def _():
        trow = lax.broadcasted_iota(I32, (t_new, LANES), 0)
        lane = lax.broadcasted_iota(I32, (t_new, LANES), 1)
        new_o[0] = jnp.where(lane <= trow, score(knew_ref[0]), -jnp.inf)


def _thr_sample_kernel(past, t_new, ksel, unroll, past_ref, new_ref, o_ref, sc_ref):
    rows = THR_ROWS
    total = past + LANES
    sc_ref[:, :past] = past_ref[...]
    sc_ref[:, past:] = new_ref[...]
    ngrp = total // (LANES * unroll)
    n_adm = past + (lax.broadcasted_iota(I32, (rows, 1), 0) & (t_new - 1)) + 1
    tb, jb = _topk_threshold(sc_ref, ngrp, rows, unroll, n_adm, ksel, (total - 1).bit_length())
    lane = lax.broadcasted_iota(I32, (rows, LANES), 1)

    def write(g, carry):
        for u in range(unroll):
            off = pl.multiple_of((g * unroll + u) * LANES, LANES)
            sel = _selected(sc_ref[:, pl.ds(off, LANES)], off, tb, jb, lane)
            o_ref[:, pl.ds(off, LANES)] = jnp.where(sel, 0.0, NEG)
        return carry

    lax.fori_loop(0, ngrp, write, 0)


def _sel_sample(pt_flat, qi_hs, wb_hs, cache_kidx, layer, knew, npg):
    db = qi_hs.shape[0]
    t_new = qi_hs.shape[1] // N_HEADS
    past = npg * PAGE
    total = past + LANES
    pps = SEL_PPS if npg % SEL_PPS == 0 else 1
    nsteps = npg // pps
    ncol = total // LANES
    unroll = max(u for u in range(1, 9) if ncol % u == 0)
    ksel = min(TOPK_KEYS, (past + t_new) // 4)
    page = lambda r: pl.BlockSpec((1, 1, IDX_DIM, PAGE), lambda b, j, pt: (layer, pt[b * npg + j * pps + r], 0, 0))
    sc_past, sc_new = pl.pallas_call(
        functools.partial(_score_sample_kernel, nsteps, pps),
        grid_spec=pltpu.PrefetchScalarGridSpec(
            num_scalar_prefetch=1, grid=(db, nsteps),
            in_specs=[pl.BlockSpec((1, N_HEADS * t_new, IDX_DIM), lambda b, j, pt: (b, 0, 0)),
                      pl.BlockSpec((1, N_HEADS * t_new, LANES), lambda b, j, pt: (b, 0, 0))]
                     + [page(r) for r in range(pps)]
                     + [pl.BlockSpec((1, IDX_DIM, PAGE), lambda b, j, pt: (b, 0, 0))],
            out_specs=[pl.BlockSpec((1, t_new, pps * PAGE), lambda b, j, pt: (b, 0, j)),
                       pl.BlockSpec((1, t_new, LANES), lambda b, j, pt: (b, 0, 0))]),
        out_shape=[jax.ShapeDtypeStruct((db, t_new, past), F32), jax.ShapeDtypeStruct((db, t_new, LANES), F32)],
        compiler_params=_cp("parallel", "arbitrary"), name="score_sample",
    )(pt_flat, qi_hs, wb_hs, *([cache_kidx] * pps), knew)
    rows = THR_ROWS
    assert (t_new & (t_new - 1)) == 0 and (db * t_new) % rows == 0
    bias = pl.pallas_call(
        functools.partial(_thr_sample_kernel, past, t_new, ksel, unroll), grid=(db * t_new // rows,),
        in_specs=[pl.BlockSpec((rows, past), lambda g: (g, 0)), pl.BlockSpec((rows, LANES), lambda g: (g, 0))],
        out_specs=pl.BlockSpec((rows, total), lambda g: (g, 0)),
        out_shape=jax.ShapeDtypeStruct((db * t_new, total), F32),
        scratch_shapes=[pltpu.VMEM((rows, total), F32)],
        compiler_params=_cp("parallel"), name="thr_sample",
    )(sc_past.reshape(db * t_new, past), sc_new.reshape(db * t_new, LANES))
    return bias.reshape(db, t_new, total)


def _paged_kernel(mode, nsteps, pps, pt_ref, q_ref, *refs):
    k_refs, v_refs, b_refs = refs[:pps], refs[pps:2 * pps], refs[2 * pps:3 * pps]
    kn_ref, vn_ref, bn_ref, o_ref, acc_ref, m_ref, l_ref, coff_ref = refs[3 * pps:]
    j = pl.program_id(1)
    t_new = o_ref.shape[2]
    rows = N_HEADS * t_new
    lane = lax.broadcasted_iota(I32, (N_HEADS, LANES), 1)

    @pl.when(j == 0)
    def _():
        acc_ref[...] = jnp.zeros(acc_ref.shape, F32)
        m_ref[...] = jnp.full(m_ref.shape, NEG, F32)
        l_ref[...] = jnp.zeros(l_ref.shape, F32)
        coff_ref[...] = jnp.zeros(coff_ref.shape, F32)

    def flat(ref_page):
        return ref_page.reshape(W_ATT, PAGE).astype(BF16)

    def attend(kt, vt, bias):
        n = kt.shape[1]
        s = _dot(q_ref[0], kt) + bias
        m_prev = m_ref[...]
        m_new = jnp.maximum(m_prev, jnp.max(s, axis=1, keepdims=True))
        alpha = jnp.exp2(m_prev - m_new)
        p = jnp.exp2(s - jnp.tile(m_new, (1, n // LANES)))
        l_ref[...] = alpha * l_ref[...] + jnp.sum(p, axis=1, keepdims=True)
        m_ref[...] = m_new
        acc_ref[...] = jnp.tile(alpha, (1, W_ATT // LANES)) * acc_ref[...] + _dot_nt(p.astype(BF16), vt)

    def neg_cum(lf):
        c = _lane_cumsum(lf, lane) + coff_ref[...]
        coff_ref[...] = jnp.broadcast_to(c[:, LANES - 1:LANES], (N_HEADS, LANES))
        return c * -LOG2E

    def per_head_rows(x):
        return jnp.broadcast_to(x[:, None, :], (N_HEADS, t_new, x.shape[1])).reshape(rows, x.shape[1])

    def per_token_rows(x):
        return jnp.tile(x, (N_HEADS, 1))

    if mode == "fox":
        bias = jnp.concatenate([per_head_rows(neg_cum(br[0, 0])) for br in b_refs], axis=1)
    else:
        bias = jnp.concatenate([per_token_rows(br[0]) for br in b_refs], axis=1)
    attend(jnp.concatenate([flat(kr[0, 0]) for kr in k_refs], axis=1),
           jnp.concatenate([flat(vr[0, 0]) for vr in v_refs], axis=1), bias)

    @pl.when(j == nsteps - 1)
    def _():
        if mode == "fox":
            tok = lax.broadcasted_iota(I32, (N_HEADS, t_new, LANES), 1).reshape(rows, LANES)
            causal = lax.broadcasted_iota(I32, (rows, LANES), 1) <= tok
            bias_new = jnp.where(causal, per_head_rows(neg_cum(bn_ref[0])), NEG)
        else:
            bias_new = per_token_rows(bn_ref[0])
        attend(flat(kn_ref[0]), flat(vn_ref[0]), bias_new)
        for h in range(N_HEADS):
            rs = slice(t_new * h, t_new * (h + 1))
            o_ref[0, h] = (acc_ref[rs, HEAD_DIM * h:HEAD_DIM * (h + 1)] / l_ref[rs, :HEAD_DIM]).astype(o_ref.dtype)


def _block_diag_q(q_hs):
    db, nh, t_new, d = q_hs.shape
    eye = jnp.eye(nh, dtype=q_hs.dtype)
    return (q_hs[:, :, :, None, :] * eye[None, :, None, :, None]).reshape(db, nh * t_new, nh * d)


def _paged(mode, pt_flat, q_hs, cache_k, cache_v, layer, bias_src, knew, vnew, bias_new, npg):
    db, _, t_new, _ = q_hs.shape
    assert t_new == N_HEADS
    pps = ATT_PPS if npg % ATT_PPS == 0 else 1
    nsteps = npg // pps
    rows = N_HEADS * t_new
    pidx = lambda r: (lambda b, j, pt: (layer, pt[b * npg + j * pps + r], 0, 0, 0))
    page = lambda r: pl.BlockSpec((1, 1, N_HEADS, HEAD_DIM, PAGE), pidx(r))
    if mode == "fox":
        bspec = lambda r: pl.BlockSpec((1, 1, N_HEADS, PAGE), lambda b, j, pt: (layer, pt[b * npg + j * pps + r], 0, 0))
        bnew = pl.BlockSpec((1, N_HEADS, LANES), lambda b, j, pt: (b, 0, 0))
    else:
        bspec = lambda r: pl.BlockSpec((1, t_new, PAGE), lambda b, j, pt: (b, 0, j * pps + r))
        bnew = pl.BlockSpec((1, t_new, LANES), lambda b, j, pt: (b, 0, npg))
    newp = pl.BlockSpec((1, N_HEADS, HEAD_DIM, PAGE), lambda b, j, pt: (b, 0, 0, 0))
    qs = pl.BlockSpec((1, rows, W_ATT), lambda b, j, pt: (b, 0, 0))
    return pl.pallas_call(
        functools.partial(_paged_kernel, mode, nsteps, pps),
        grid_spec=pltpu.PrefetchScalarGridSpec(
            num_scalar_prefetch=1, grid=(db, nsteps),
            in_specs=[qs] + [page(r) for r in range(pps)] + [page(r) for r in range(pps)]
                     + [bspec(r) for r in range(pps)] + [newp, newp, bnew],
            out_specs=pl.BlockSpec((1, N_HEADS, t_new, HEAD_DIM), lambda b, j, pt: (b, 0, 0, 0)),
            scratch_shapes=[pltpu.VMEM((rows, W_ATT), F32), pltpu.VMEM((rows, LANES), F32),
                            pltpu.VMEM((rows, LANES), F32), pltpu.VMEM((N_HEADS, LANES), F32)]),
        out_shape=jax.ShapeDtypeStruct(q_hs.shape, BF16),
        compiler_params=_cp("parallel", "arbitrary"), name="paged_" + mode,
    )(pt_flat, _block_diag_q(q_hs), *([cache_k] * pps), *([cache_v] * pps), *([bias_src] * pps), knew, vnew, bias_new)


def _blockout_kernel(x_ref, oa_ref, ob_ref, om_ref, gmix_ref, wg_ref, wpa_ref, wpb_ref, wpm_ref, wo_ref, gffn_ref,
                     wr_ref, br_ref, x1_o, h2_o, te_o, tg_o):
    x = x_ref[...]
    tm = x.shape[0]
    h = _rms_rows(x, gmix_ref[...]).astype(BF16)
    merged = jnp.zeros((tm, D_MODEL), F32)
    for n, (o_ref, wp_ref) in enumerate(((oa_ref, wpa_ref), (ob_ref, wpb_ref), (om_ref, wpm_ref))):
        gate = jax.nn.sigmoid(_dot(h, wg_ref[:, D_MODEL * n:D_MODEL * (n + 1)]))
        merged = merged + gate * _dot(o_ref[...], wp_ref[...])
    x1 = x + _dot(merged.astype(BF16), wo_ref[...])
    x1_o[...] = x1
    h2 = _rms_rows(x1, gffn_ref[...])
    for c in range(D_MODEL // LANES):
        h2_o[:, c, :] = h2[:, LANES * c:LANES * (c + 1)]
    logits = _dot(h2.astype(BF16), wr_ref[...]) + br_ref[...]
    lane = lax.broadcasted_iota(I32, (tm, LANES), 1)
    vals, te = [], jnp.zeros((tm, LANES), I32)
    for k in range(TOP_K):
        m = jnp.max(logits, axis=1, keepdims=True)
        ix = jnp.min(jnp.where(logits == m, lane, LANES), axis=1, keepdims=True)
        vals.append(m)
        te = jnp.where(lane == k, ix, te)
        logits = jnp.where(lane == ix, -jnp.inf, logits)
    ex = [jnp.exp(v - vals[0]) for v in vals]
    den = ex[0] + ex[1] + ex[2] + ex[3]
    tg = jnp.zeros((tm, LANES), F32)
    for k in range(TOP_K):
        tg = jnp.where(lane == k, ex[k] / den, tg)
    te_o[...] = te
    tg_o[...] = tg


def _blockout(x2d, oa, ob, om, wts):
    n = x2d.shape[0]
    tm = OUT_TM
    row = lambda w: pl.BlockSpec((tm, w), lambda i: (i, 0))
    names = ("g_mix", "wg", "wpa", "wpb", "wpm", "wo", "g_ffn", "wr", "br")
    nch = D_MODEL // LANES
    return pl.pallas_call(
        _blockout_kernel, grid=(n // tm,),
        in_specs=[row(D_MODEL), row(W_ATT), row(W_ATT), row(N_HEADS_M * HEAD_DIM_M)] + [_full_spec(wts[k]) for k in names],
        out_specs=[row(D_MODEL), pl.BlockSpec((tm, nch, LANES), lambda i: (i, 0, 0)), row(LANES), row(LANES)],
        out_shape=[jax.ShapeDtypeStruct((n, D_MODEL), F32), jax.ShapeDtypeStruct((n, nch, LANES), F32),
                   jax.ShapeDtypeStruct((n, LANES), I32), jax.ShapeDtypeStruct((n, LANES), F32)],
        compiler_params=_cp("parallel"), name="blockout")(x2d, oa, ob, om, *[wts[k] for k in names])


def _row_copy(src_ref, dst_ref, src_row, dst_row, sem):
    return pltpu.make_async_copy(src_ref.at[src_row], dst_ref.at[dst_row], sem)


def _gather_kernel(idx_ref, src_ref, out_ref, sem):
    def issue(r, c):
        _row_copy(src_ref, out_ref, idx_ref[r], r, sem).start()
        return c

    lax.fori_loop(0, GATHER_R, issue, 0)

    def wait(r, c):
        _row_copy(src_ref, out_ref, 0, r, sem).wait()
        return c

    lax.fori_loop(0, GATHER_R, wait, 0)


def _gather_rows(idx, src):
    n = idx.shape[0]
    return pl.pallas_call(
        _gather_kernel, grid=(n // GATHER_R,),
        in_specs=[pl.BlockSpec((GATHER_R,), lambda i: (i,), memory_space=pltpu.SMEM),
                  pl.BlockSpec(memory_space=pl.ANY)],
        out_specs=pl.BlockSpec((GATHER_R,) + src.shape[1:], lambda i: (i, 0, 0)),
        out_shape=jax.ShapeDtypeStruct((n,) + src.shape[1:], src.dtype),
        scratch_shapes=[pltpu.SemaphoreType.DMA(())],
        compiler_params=_cp("arbitrary"), name="moe_gather")(idx, src)


def _ffn_kernel(be_ref, na_ref, x_ref, wu_ref, bu_ref, wd_ref, bd_ref, y_ref):
    nch = D_MODEL // LANES
    i = pl.program_id(0)

    @pl.when(i < na_ref[0])
    def _():
        x = jnp.concatenate([x_ref[:, c, :] for c in range(nch)], axis=1).astype(BF16)
        u = _dot(x, wu_ref[0]) + bu_ref[0]
        glu = jnp.minimum(u[:, :D_FF], SWIGLU_LIMIT)
        lin = jnp.clip(u[:, D_FF:], -SWIGLU_LIMIT, SWIGLU_LIMIT)
        a = glu * jax.nn.sigmoid(SWIGLU_ALPHA * glu) * (lin + 1.0)
        y = _dot(a.astype(BF16), wd_ref[0]) + bd_ref[0]
        for c in range(nch):
            y_ref[:, c, :] = y[:, LANES * c:LANES * (c + 1)]

    @pl.when(i >= na_ref[0])
    def _():
        y_ref[...] = jnp.zeros(y_ref.shape, F32)


def _ffn(blk_expert, n_active, xs, w_up, b_up, w_down, b_down):
    n = xs.shape[0]
    nch = D_MODEL // LANES
    xspec = pl.BlockSpec((MOE_BLK, nch, LANES), lambda i, be, na: (i, 0, 0))
    ex = lambda shape: pl.BlockSpec((1,) + shape, lambda i, be, na: (be[i], 0, 0))
    return pl.pallas_call(
        _ffn_kernel,
        grid_spec=pltpu.PrefetchScalarGridSpec(
            num_scalar_prefetch=2, grid=(n // MOE_BLK,),
            in_specs=[xspec, ex((D_MODEL, 2 * D_FF)), ex((1, 2 * D_FF)), ex((D_FF, D_MODEL)), ex((1, D_MODEL))],
            out_specs=xspec),
        out_shape=jax.ShapeDtypeStruct(xs.shape, F32),
        compiler_params=_cp("arbitrary"), name="moe_ffn")(blk_expert, n_active, xs, w_up, b_up, w_down, b_down)


def _combine_kernel(idx_ref, ys_ref, x1_ref, g_ref, o_ref, buf_ref, sem):
    tm = COMB_TM
    nch = D_MODEL // LANES

    def issue(r, c):
        _row_copy(ys_ref, buf_ref, idx_ref[r], r, sem).start()
        return c

    lax.fori_loop(0, TOP_K * tm, issue, 0)

    def wait(r, c):
        _row_copy(ys_ref, buf_ref, 0, r, sem).wait()
        return c

    lax.fori_loop(0, TOP_K * tm, wait, 0)
    g = g_ref[...]
    y = x1_ref[...]
    for k in range(TOP_K):
        rows = jnp.concatenate([buf_ref[k * tm:(k + 1) * tm, c, :] for c in range(nch)], axis=1)
        y = y + g[:, k:k + 1] * rows
    o_ref[...] = y


def _combine(dest_tiles, ys, x1, gates):
    n = x1.shape[0]
    tm = COMB_TM
    row = lambda w: pl.BlockSpec((tm, w), lambda i: (i, 0))
    return pl.pallas_call(
        _combine_kernel, grid=(n // tm,),
        in_specs=[pl.BlockSpec((TOP_K * tm,), lambda i: (i,), memory_space=pltpu.SMEM),
                  pl.BlockSpec(memory_space=pl.ANY), row(D_MODEL), row(LANES)],
        out_specs=row(D_MODEL),
        out_shape=jax.ShapeDtypeStruct((n, D_MODEL), F32),
        scratch_shapes=[pltpu.VMEM((TOP_K * tm,) + ys.shape[1:], F32), pltpu.SemaphoreType.DMA(())],
        compiler_params=_cp("arbitrary"), name="moe_combine")(dest_tiles, ys, x1, gates)


def _moe(x1, h2, top_e, top_g, wts):
    n = x1.shape[0]
    n_assign = n * TOP_K
    flat_e = top_e[:, :TOP_K].reshape(-1)
    onehot = (flat_e[:, None] == jnp.arange(N_EXPERTS, dtype=I32)[None, :]).astype(I32)
    csum = jnp.cumsum(onehot, axis=0)
    rank = jnp.take_along_axis(csum, flat_e[:, None], axis=1)[:, 0] - 1
    counts = csum[-1]
    padded = (counts + MOE_BLK - 1) // MOE_BLK * MOE_BLK
    pad_end = jnp.cumsum(padded)
    dest = (pad_end - padded)[flat_e] + rank
    n_blocks = -(-n_assign // MOE_BLK) + N_EXPERTS
    n_rows = -(-(n_blocks * MOE_BLK) // GATHER_R) * GATHER_R
    row_tok = jnp.zeros((n_rows,), I32).at[dest].set(jnp.arange(n_assign, dtype=I32) // TOP_K)
    nb = n_rows // MOE_BLK
    blk_expert = jnp.minimum(jnp.searchsorted(pad_end, jnp.arange(nb, dtype=I32) * MOE_BLK, side="right"),
                             N_EXPERTS - 1).astype(I32)
    n_active = (pad_end[-1:] // MOE_BLK).astype(I32)
    xs = _gather_rows(row_tok, h2)
    ys = _ffn(blk_expert, n_active, xs, wts["w_up"], wts["b_up"], wts["w_down"], wts["b_down"])
    dest_tiles = dest.reshape(n // COMB_TM, COMB_TM, TOP_K).transpose(0, 2, 1).reshape(-1)
    return _combine(dest_tiles, ys, x1, top_g)


def _block_diag(width, blk):
    r = jnp.arange(width) // blk
    return jnp.where(r[:, None] == r[None, :], 1.0 / blk, 0.0).astype(BF16)


def _pad_cols(a, width):
    return jnp.pad(a, ((0, 0), (0, width - a.shape[1])))


def _layer_weights(l, w_in, b_forget, g_mix, g_qa, g_ka, g_kidx, g_qb, g_kb, g_qm, g_mem, w_mem_kv, g_km, w_pa, w_pb,
                   w_pm, w_o, g_ffn, w_router, b_router, w_up, b_up, w_down, b_down):
    pts = [0]
    for s in SPLIT_SIZES:
        pts.append(pts[-1] + s)
    col = lambda n: w_in[l][:, pts[n]:pts[n + 1]]
    qa, ka, va, qi, ki, wi, qb, kb, vb, fb, qm, ga, gb, gm = [col(n) for n in range(len(SPLIT_SIZES))]
    rowv = lambda v: v.astype(F32)[None, :]
    scale = HEAD_DIM ** -0.5 * LOG2E
    return dict(
        wbig=jnp.concatenate([qa, ka, va, qi, qb, kb, vb, qm], axis=1).astype(BF16),
        wsm=jnp.concatenate([_pad_cols(ki, LANES), _pad_cols(wi, LANES), _pad_cols(fb, LANES)], axis=1).astype(BF16),
        wg=jnp.concatenate([ga, gb, gm], axis=1).astype(BF16),
        g_mix=rowv(g_mix[l]), g_ffn=rowv(g_ffn[l]), g_mem=rowv(g_mem[l]),
        gqa=rowv(jnp.tile(g_qa[l] * scale, N_HEADS)), gka=rowv(jnp.tile(g_ka[l], N_HEADS)),
        gqb=rowv(jnp.tile(g_qb[l] * scale, N_HEADS)), gkb=rowv(jnp.tile(g_kb[l], N_HEADS)),
        gqm=rowv(jnp.tile(g_qm[l], N_HEADS_M)), gkm=rowv(jnp.tile(g_km[l], N_HEADS_M)),
        gki=_pad_cols(rowv(g_kidx[l]), LANES), bf=_pad_cols(rowv(b_forget[l]), LANES),
        bd64=_block_diag(W_ATT, HEAD_DIM), bd128=_block_diag(N_HEADS_M * HEAD_DIM_M, HEAD_DIM_M),
        w_mem_kv=w_mem_kv[l].astype(BF16),
        wpa=w_pa[l].astype(BF16), wpb=w_pb[l].astype(BF16), wpm=w_pm[l].astype(BF16), wo=w_o[l].astype(BF16),
        wr=_pad_cols(w_router[l], LANES).astype(BF16),
        br=jnp.pad(rowv(b_router[l]), ((0, 0), (0, LANES - N_EXPERTS)), constant_values=NEG),
        w_up=w_up[l].astype(BF16), b_up=b_up[l].astype(F32)[:, None, :],
        w_down=w_down[l].astype(BF16), b_down=b_down[l].astype(F32)[:, None, :],
    )


def _finish_block(x2d, oa, ob, om, wts):
    x1, h2, te, tg = _blockout(x2d, oa, ob, om, wts)
    return _moe(x1, h2, te, tg, wts)


def kernel(x_prompt, x_sample, mem_prompt, cache_a_k, cache_a_v, cache_a_kidx, cache_b_k, cache_b_v, cache_b_logf,
           cache_mem_k, cache_mem_v, page_table, w_in, b_forget, g_mix, g_qa, g_ka, g_kidx, g_qb, g_kb, g_qm, g_mem,
           w_mem_kv, g_km, w_pa, w_pb, w_pm, w_o, g_ffn, w_router, b_router, w_up, b_up, w_down, b_down):
    depth = w_in.shape[0]
    b, seq, _ = x_prompt.shape
    db, t_new, _ = x_sample.shape
    npg = page_table.shape[1]
    past = npg * PAGE
    pt_flat = page_table.reshape(-1).astype(I32)
    tabs_p = _rope_tables(jnp.arange(seq))
    tabs_s = _rope_tables(past + jnp.arange(PROJ_TM) % t_new)
    page_t = lambda c: jnp.transpose(c, (0, 1, 3, 4, 2))
    cak_t, cav_t, cbk_t, cbv_t = page_t(cache_a_k), page_t(cache_a_v), page_t(cache_b_k), page_t(cache_b_v)
    kidx_t = jnp.swapaxes(cache_a_kidx, 2, 3)
    logf_cache_t = jnp.swapaxes(cache_b_logf, 2, 3)
    hp = x_prompt.reshape(b * seq, D_MODEL)
    hs = x_sample.reshape(db * t_new, D_MODEL)
    p_states, s_states = [], []
    heads = lambda t, nb, nt: t.reshape(nb, nt, N_HEADS, HEAD_DIM)
    for l in range(depth):
        wts = _layer_weights(l, w_in, b_forget, g_mix, g_qa, g_ka, g_kidx, g_qb, g_kb, g_qm, g_mem, w_mem_kv, g_km,
                             w_pa, w_pb, w_pm, w_o, g_ffn, w_router, b_router, w_up, b_up, w_down, b_down)
        pr = _proj(hp, tabs_p, seq // PROJ_TM, wts)
        mk, mv = _memkv(mem_prompt.reshape(b * N_MEM, D_MODEL), wts["g_mem"], wts["w_mem_kv"], wts["gkm"], wts["bd128"])
        mk5 = mk.reshape(1, b, N_MEM, N_HEADS_M, HEAD_DIM_M)
        mv5 = mv.reshape(1, b, N_MEM, N_HEADS_M, HEAD_DIM_M)
        r3 = lambda t: t.reshape(b, seq, t.shape[-1])
        sel_bias = _sel_prompt(r3(pr["qi"]), r3(pr["wi"]), r3(pr["ki2"]))
        o_a = _flash("dsa", r3(pr["qa"]), r3(pr["ka_bf"]), r3(pr["va_bf"]), sel_bias)
        negc = _negcumsum(jnp.swapaxes(r3(pr["logf"]), 1, 2))
        o_b = _flash("fox", r3(pr["qb"]), r3(pr["kb_bf"]), r3(pr["vb_bf"]), negc)
        o_m = _memattn(r3(pr["qm"]), mk5, mv5, 0, MEM_TQ)
        rf = lambda t: t.reshape(b * seq, t.shape[-1])
        p_states.append((heads(pr["ka"], b, seq), heads(pr["va"], b, seq), r3(pr["ki"]), heads(pr["kb"], b, seq),
                         heads(pr["vb"], b, seq), r3(pr["logf"]), mk5[0], mv5[0]))
        hp = _finish_block(hp, rf(o_a), rf(o_b), rf(o_m), wts)
        sr = _proj(hs, tabs_s, 1, wts)
        hs_t = lambda t, d: jnp.swapaxes(t.reshape(db, t_new, N_HEADS, d), 1, 2)
        new_page = lambda t: jnp.pad(jnp.transpose(heads(t, db, t_new), (0, 2, 3, 1)),
                                     ((0, 0), (0, 0), (0, 0), (0, PAGE - t_new)))
        qi_hs = hs_t(sr["qi"], IDX_DIM).reshape(db, N_HEADS * t_new, IDX_DIM)
        wb_hs = jnp.broadcast_to(jnp.swapaxes(sr["wi"].reshape(db, t_new, N_HEADS), 1, 2).reshape(db, N_HEADS * t_new, 1),
                                 (db, N_HEADS * t_new, LANES))
        ki_new = jnp.pad(jnp.swapaxes(sr["ki"].reshape(db, t_new, IDX_DIM), 1, 2), ((0, 0), (0, 0), (0, PAGE - t_new)))
        sel_s = _sel_sample(pt_flat, qi_hs, wb_hs, kidx_t, l, ki_new, npg)
        o_a = _paged("dsa", pt_flat, hs_t(sr["qa"], HEAD_DIM), cak_t, cav_t, l, sel_s,
                     new_page(sr["ka"]), new_page(sr["va"]), sel_s, npg)
        lf_new = jnp.pad(jnp.swapaxes(sr["logf"].reshape(db, t_new, N_HEADS), 1, 2), ((0, 0), (0, 0), (0, LANES - t_new)))
        o_b = _paged("fox", pt_flat, hs_t(sr["qb"], HEAD_DIM), cbk_t, cbv_t, l, logf_cache_t,
                     new_page(sr["kb"]), new_page(sr["vb"]), lf_new, npg)
        o_m = _memattn(sr["qm"].reshape(db, t_new, -1), cache_mem_k, cache_mem_v, l, t_new)
        un_hs = lambda o: jnp.swapaxes(o, 1, 2).reshape(db * t_new, W_ATT)
        s_states.append((heads(sr["ka"], db, t_new), heads(sr["va"], db, t_new), sr["ki"].reshape(db, t_new, IDX_DIM),
                         heads(sr["kb"], db, t_new), heads(sr["vb"], db, t_new), sr["logf"].reshape(db, t_new, N_HEADS)))
        hs = _finish_block(hs, un_hs(o_a), un_hs(o_b), o_m.reshape(db * t_new, -1), wts)
    stack = lambda states: [jnp.stack(s, 0) for s in zip(*states)]
    return (hp.reshape(b, seq, D_MODEL), hs.reshape(db, t_new, D_MODEL), *stack(p_states), *stack(s_states))
```
